```python
import jax
import jax.numpy as jnp
from jax import lax
import numpy as np

D_MODEL = 1024
BATCH = 2
SEQ = 8192
DEPTH = 1
DEC_BATCH = 128
DEC_SEQ = 8
PAST_LEN = 2048
PAGE_SIZE = 128

H_GLA = 4
GLA_VAL_W = D_MODEL // 2
GLA_DV = GLA_VAL_W // H_GLA
GLA_DK = GLA_DV // 2
GLA_KEY_W = H_GLA * GLA_DK
GATE_RANK = 16
GATE_TAU = 16.0
GLA_CHUNK = 64
H_DIL = 4
DIL_W = D_MODEL - GLA_VAL_W
DIL_DH = DIL_W // H_DIL
DIL_PAIRS = ((128, 1), (512, 4), (2048, 16))
WINDOW_MAX = 2048
DIL_BLOCK = 128
MIX_W = GLA_VAL_W + DIL_W
PROJ_WIDTHS = (GLA_KEY_W, GLA_KEY_W, GLA_VAL_W, GLA_VAL_W, GATE_RANK, DIL_W, DIL_W, DIL_W)
PROJ_DIM = sum(PROJ_WIDTHS)
MEM_TOKENS = 256
MEM_HEADS = 4
MEM_DH = D_MODEL // MEM_HEADS
N_GROUPS = 4
EXPERTS_PER_GROUP = 8
N_EXPERTS = N_GROUPS * EXPERTS_PER_GROUP
TOP_K = 2
EXPERT_HIDDEN = D_MODEL // 2
ALPHA = (2.0 * DEPTH) ** 0.25
BETA = (8.0 * DEPTH) ** -0.25
LN_EPS = 1e-5
NEG_INF = -1e30

kernel_name = 'hymba_gla_dilated_hmoe_step'


def layer_norm(x, g, b):
    xf = x.astype(jnp.float32)
    mu = jnp.mean(xf, -1, keepdims=True)
    var = jnp.mean(jnp.square(xf - mu), -1, keepdims=True)
    return ((xf - mu) * lax.rsqrt(var + LN_EPS) * g + b).astype(x.dtype)


def project_mixer_inputs(x, w_in, w_gate_lr, b_gate):
    B, S, _ = x.shape
    pts, acc = [], 0
    for w in PROJ_WIDTHS[:-1]:
        acc += w
        pts.append(acc)
    h = x @ w_in
    q_g, k_g, v_g, r_g, a_lr, q_d, k_d, v_d = jnp.split(h, pts, axis=-1)
    q_g = q_g.reshape(B, S, H_GLA, GLA_DK) * (GLA_DK ** -0.5)
    k_g = k_g.reshape(B, S, H_GLA, GLA_DK)
    v_g = v_g.reshape(B, S, H_GLA, GLA_DV)
    log_a = jax.nn.log_sigmoid((a_lr @ w_gate_lr + b_gate).astype(jnp.float32)) / GATE_TAU
    log_a = log_a.reshape(B, S, H_GLA, GLA_DK)
    q_d = q_d.reshape(B, S, H_DIL, DIL_DH)
    k_d = k_d.reshape(B, S, H_DIL, DIL_DH)
    v_d = v_d.reshape(B, S, H_DIL, DIL_DH)
    return q_g, k_g, v_g, r_g, log_a, q_d, k_d, v_d


def gla_chunked(q, k, v, log_a, s0, chunk):
    B, S, H, dk = q.shape
    dv = v.shape[-1]
    n = S // chunk

    def to_chunks(t):
        return t.astype(jnp.float32).reshape(B, n, chunk, H, t.shape[-1]).transpose(1, 0, 3, 2, 4)

    qc, kc, vc, ac = to_chunks(q), to_chunks(k), to_chunks(v), to_chunks(log_a)
    causal = jnp.tril(jnp.ones((chunk, chunk), dtype=bool))

    def step(s, inp):
        qi, ki, vi, ai = inp
        b = jnp.cumsum(ai, axis=2)
        b_last = b[:, :, -1:, :]
        q_t = qi * jnp.exp(b)
        k_t = ki * jnp.exp(-b)
        a_in = jnp.where(causal, jnp.einsum('bhik,bhjk->bhij', q_t, k_t), 0.0)
        o = jnp.einsum('bhij,bhjv->bhiv', a_in, vi) + jnp.einsum('bhik,bhkv->bhiv', q_t, s)
        k_end = ki * jnp.exp(b_last - b)
        s_new = jnp.exp(b_last[:, :, 0, :])[..., None] * s + jnp.einsum('bhjk,bhjv->bhkv', k_end, vi)
        return s_new, o

    s_fin, o = lax.scan(step, s0.astype(jnp.float32), (qc, kc, vc, ac))
    o = o.transpose(1, 0, 3, 2, 4).reshape(B, S, H, dv)
    return o, s_fin


def combine_by_denominator(outs, lses):
    w = jax.nn.softmax(jnp.stack(lses, 0), axis=0)
    return jnp.sum(w[..., None] * jnp.stack(outs, 0), axis=0)


def dilated_attention_prompt(q, k, v):
    B, S, H, dh = q.shape
    scale = dh ** -0.5
    outs, lses = [], []
    for window, dil in DIL_PAIRS:
        span = window // dil
        L = S // dil
        nb = -(-L // DIL_BLOCK)
        pad = nb * DIL_BLOCK - L

        def to_sub(t):
            t = t.reshape(B, L, dil, H, dh).transpose(0, 2, 3, 1, 4)
            t = jnp.pad(t, ((0, 0), (0, 0), (0, 0), (0, pad), (0, 0)))
            return t.reshape(B, dil, H, nb, DIL_BLOCK, dh)

        def with_prev(t):
            prev = jnp.pad(t[:, :, :, :-1], ((0, 0), (0, 0), (0, 0), (1, 0), (0, 0), (0, 0)))
            return jnp.concatenate([prev, t], axis=4)

        qs = to_sub(q)
        kb = with_prev(to_sub(k))
        vb = with_prev(to_sub(v))
        s = jnp.einsum('bdhnqc,bdhnkc->bdhnqk', qs, kb).astype(jnp.float32) * scale
        qi = jnp.arange(DIL_BLOCK)[:, None]
        ki = jnp.arange(2 * DIL_BLOCK)[None, :]
        dist = qi + DIL_BLOCK - ki
        key_pos = jnp.arange(nb)[:, None, None] * DIL_BLOCK + ki[None] - DIL_BLOCK
        valid = (dist >= 0) & (dist <= span) & (key_pos >= 0)
        s = jnp.where(valid, s, NEG_INF)
        m = jnp.max(s, -1, keepdims=True)
        p = jnp.exp(s - m)
        den = jnp.sum(p, -1, keepdims=True)
        o = jnp.einsum('bdhnqk,bdhnkc->bdhnqc', p, vb.astype(jnp.float32)) / den
        lse = (m + jnp.log(den))[..., 0]
        o = o.reshape(B, dil, H, nb * DIL_BLOCK, dh)[:, :, :, :L].transpose(0, 3, 1, 2, 4).reshape(B, S, H, dh)
        lse = lse.reshape(B, dil, H, nb * DIL_BLOCK)[..., :L].transpose(0, 3, 1, 2).reshape(B, S, H)
        outs.append(o)
        lses.append(lse)
    return combine_by_denominator(outs, lses)


def dilated_attention_sample(q, k_new, v_new, k_buf, v_buf):
    DB, T, H, dh = q.shape
    scale = dh ** -0.5
    buf_len = k_buf.shape[1]
    k_all = jnp.concatenate([k_buf, k_new], axis=1)
    v_all = jnp.concatenate([v_buf, v_new], axis=1)
    outs, lses = [], []
    for window, dil in DIL_PAIRS:
        span = window // dil
        idx = buf_len + jnp.arange(T)[:, None] - dil * jnp.arange(span + 1)[None, :]
        valid = idx >= 0
        flat = jnp.maximum(idx, 0).reshape(-1)
        kg = jnp.take(k_all, flat, axis=1).reshape(DB, T, span + 1, H, dh)
        vg = jnp.take(v_all, flat, axis=1).reshape(DB, T, span + 1, H, dh)
        s = jnp.einsum('bthc,btkhc->bthk', q, kg).astype(jnp.float32) * scale
        s = jnp.where(valid[None, :, None, :], s, NEG_INF)
        m = jnp.max(s, -1, keepdims=True)
        p = jnp.exp(s - m)
        den = jnp.sum(p, -1, keepdims=True)
        o = jnp.einsum('bthk,btkhc->bthc', p, vg.astype(jnp.float32)) / den
        outs.append(o)
        lses.append((m + jnp.log(den))[..., 0])
    new_len = min(WINDOW_MAX, k_all.shape[1])
    return combine_by_denominator(outs, lses), k_all[:, -new_len:], v_all[:, -new_len:]


def mixer_output(o_gla, r_g, o_dil, g_gla, w_out):
    B, S = o_gla.shape[:2]
    on = o_gla * lax.rsqrt(jnp.mean(jnp.square(o_gla), -1, keepdims=True) + LN_EPS)
    on = on.reshape(B, S, GLA_VAL_W) * g_gla * jax.nn.silu(r_g.astype(jnp.float32))
    cat = jnp.concatenate([on.astype(r_g.dtype), o_dil.reshape(B, S, DIL_W).astype(r_g.dtype)], axis=-1)
    return cat @ w_out


def memory_attention(x, mem_k, mem_v, w_mem_q, w_mem_o):
    B, S, _ = x.shape
    q = (x @ w_mem_q).reshape(B, S, MEM_HEADS, MEM_DH)
    s = jnp.einsum('bshc,bmhc->bhsm', q, mem_k).astype(jnp.float32) * (MEM_DH ** -0.5)
    p = jax.nn.softmax(s, axis=-1).astype(mem_v.dtype)
    o = jnp.einsum('bhsm,bmhc->bshc', p, mem_v).reshape(B, S, D_MODEL)
    return o @ w_mem_o


def hierarchical_moe(x, w_route_group, b_route_group, w_route_expert, b_route_expert, w_exp_gate, w_exp_up, w_exp_down):
    B, S, D = x.shape
    xt = x.reshape(-1, D)
    g_logits = (xt @ w_route_group).astype(jnp.float32) + b_route_group
    g_prob = jax.nn.softmax(g_logits, axis=-1)
    g_sel = jnp.argmax(g_logits, axis=-1)
    g_w = jnp.take_along_axis(g_prob, g_sel[:, None], axis=-1)
    e_logits = ((xt @ w_route_expert).astype(jnp.float32) + b_route_expert).reshape(-1, N_GROUPS, EXPERTS_PER_GROUP)
    e_in = jnp.take_along_axis(e_logits, g_sel[:, None, None], axis=1)[:, 0]
    top_v, top_i = lax.top_k(e_in, TOP_K)
    top_w = jax.nn.softmax(top_v, axis=-1) * g_w
    expert_id = g_sel[:, None] * EXPERTS_PER_GROUP + top_i
    gate = jnp.sum(jax.nn.one_hot(expert_id, N_EXPERTS, dtype=jnp.float32) * top_w[..., None], axis=1)
    gate = gate.astype(x.dtype)
    y = jnp.zeros_like(xt)
    for e in range(N_EXPERTS):
        h = jax.nn.silu(xt @ w_exp_gate[e]) * (xt @ w_exp_up[e])
        y = y + gate[:, e:e + 1] * (h @ w_exp_down[e])
    return y.reshape(B, S, D)


def post_mixer(x, mix, mem_k, mem_v, ln_mix_g, ln_mix_b, w_mem_q, w_mem_o, ln_mem_g, ln_mem_b,
               w_route_group, b_route_group, w_route_expert, b_route_expert, w_exp_gate, w_exp_up, w_exp_down,
               ln_ffn_g, ln_ffn_b):
    x = layer_norm(ALPHA * x + mix, ln_mix_g, ln_mix_b)
    x = layer_norm(ALPHA * x + memory_attention(x, mem_k, mem_v, w_mem_q, w_mem_o), ln_mem_g, ln_mem_b)
    moe = hierarchical_moe(x, w_route_group, b_route_group, w_route_expert, b_route_expert, w_exp_gate, w_exp_up, w_exp_down)
    return layer_norm(ALPHA * x + moe, ln_ffn_g, ln_ffn_b)


def setup_inputs(seed: int = 0) -> dict:
    key = jax.random.key(seed)
    ks = jax.random.split(key, 30)

    def nrm(k, shape, scale=1.0):
        return jax.random.normal(k, shape, jnp.float32) * scale

    buf = min(WINDOW_MAX, PAST_LEN)
    return {
        'x_prompt': nrm(ks[0], (BATCH, SEQ, D_MODEL)),
        'x_sample': nrm(ks[1], (DEC_BATCH, DEC_SEQ, D_MODEL)),
        'mem_prompt': nrm(ks[2], (BATCH, MEM_TOKENS, D_MODEL)),
        'cache_dil_k': nrm(ks[3], (DEPTH, DEC_BATCH, buf, H_DIL, DIL_DH)),
        'cache_dil_v': nrm(ks[4], (DEPTH, DEC_BATCH, buf, H_DIL, DIL_DH)),
        'state_gla': nrm(ks[5], (DEPTH, DEC_BATCH, H_GLA, GLA_DK, GLA_DV), 0.3),
        'cache_mem_k': nrm(ks[6], (DEPTH, DEC_BATCH, MEM_TOKENS, MEM_HEADS, MEM_DH)),
        'cache_mem_v': nrm(ks[7], (DEPTH, DEC_BATCH, MEM_TOKENS, MEM_HEADS, MEM_DH)),
        'w_in': nrm(ks[8], (DEPTH, D_MODEL, PROJ_DIM), D_MODEL ** -0.5),
        'w_gate_lr': nrm(ks[9], (DEPTH, GATE_RANK, GLA_KEY_W), GATE_RANK ** -0.5),
        'b_gate': nrm(ks[10], (DEPTH, GLA_KEY_W), 0.1),
        'g_gla_norm': 1.0 + nrm(ks[11], (DEPTH, GLA_VAL_W), 0.02),
        'w_out': nrm(ks[12], (DEPTH, MIX_W, D_MODEL), BETA * MIX_W ** -0.5),
        'ln_mix_g': 1.0 + nrm(ks[13], (DEPTH, D_MODEL), 0.02),
        'ln_mix_b': nrm(ks[14], (DEPTH, D_MODEL), 0.02),
        'w_mem_q': nrm(ks[15], (DEPTH, D_MODEL, D_MODEL), D_MODEL ** -0.5),
        'w_mem_k': nrm(ks[16], (DEPTH, D_MODEL, D_MODEL), D_MODEL ** -0.5),
        'w_mem_v': nrm(ks[17], (DEPTH, D_MODEL, D_MODEL), D_MODEL ** -0.5),
        'w_mem_o': nrm(ks[18], (DEPTH, D_MODEL, D_MODEL), BETA * D_MODEL ** -0.5),
        'ln_mem_g': 1.0 + nrm(ks[19], (DEPTH, D_MODEL), 0.02),
        'ln_mem_b': nrm(ks[20], (DEPTH, D_MODEL), 0.02),
        'w_route_group': nrm(ks[21], (DEPTH, D_MODEL, N_GROUPS), D_MODEL ** -0.5),
        'b_route_group': nrm(ks[22], (DEPTH, N_GROUPS), 0.01),
        'w_route_expert': nrm(ks[23], (DEPTH, D_MODEL, N_EXPERTS), D_MODEL ** -0.5),
        'b_route_expert': nrm(ks[24], (DEPTH, N_EXPERTS), 0.01),
        'w_exp_gate': nrm(ks[25], (DEPTH, N_EXPERTS, D_MODEL, EXPERT_HIDDEN), D_MODEL ** -0.5),
        'w_exp_up': nrm(ks[26], (DEPTH, N_EXPERTS, D_MODEL, EXPERT_HIDDEN), D_MODEL ** -0.5),
        'w_exp_down': nrm(ks[27], (DEPTH, N_EXPERTS, EXPERT_HIDDEN, D_MODEL), BETA * EXPERT_HIDDEN ** -0.5),
        'ln_ffn_g': 1.0 + nrm(ks[28], (DEPTH, D_MODEL), 0.02),
        'ln_ffn_b': nrm(ks[29], (DEPTH, D_MODEL), 0.02),
    }


def reference(x_prompt, x_sample, mem_prompt, cache_dil_k, cache_dil_v, state_gla, cache_mem_k, cache_mem_v,
              w_in, w_gate_lr, b_gate, g_gla_norm, w_out, ln_mix_g, ln_mix_b,
              w_mem_q, w_mem_k, w_mem_v, w_mem_o, ln_mem_g, ln_mem_b,
              w_route_group, b_route_group, w_route_expert, b_route_expert,
              w_exp_gate, w_exp_up, w_exp_down, ln_ffn_g, ln_ffn_b):
    xp, xs = x_prompt, x_sample
    Bp = xp.shape[0]
    dk_p, dv_p, sg_p, mk_p, mv_p = [], [], [], [], []
    dk_s, dv_s, sg_s = [], [], []
    for l in range(DEPTH):
        post = (ln_mix_g[l], ln_mix_b[l], w_mem_q[l], w_mem_o[l], ln_mem_g[l], ln_mem_b[l],
                w_route_group[l], b_route_group[l], w_route_expert[l], b_route_expert[l],
                w_exp_gate[l], w_exp_up[l], w_exp_down[l], ln_ffn_g[l], ln_ffn_b[l])
        q_g, k_g, v_g, r_g, log_a, q_d, k_d, v_d = project_mixer_inputs(xp, w_in[l], w_gate_lr[l], b_gate[l])
        s0 = jnp.zeros((Bp, H_GLA, GLA_DK, GLA_DV), jnp.float32)
        o_g, s_fin = gla_chunked(q_g, k_g, v_g, log_a, s0, GLA_CHUNK)
        o_d = dilated_attention_prompt(q_d, k_d, v_d)
        mix = mixer_output(o_g, r_g, o_d, g_gla_norm[l], w_out[l])
        mem_k = (mem_prompt @ w_mem_k[l]).reshape(Bp, MEM_TOKENS, MEM_HEADS, MEM_DH)
        mem_v = (mem_prompt @ w_mem_v[l]).reshape(Bp, MEM_TOKENS, MEM_HEADS, MEM_DH)
        xp = post_mixer(xp, mix, mem_k, mem_v, *post)
        keep = min(WINDOW_MAX, k_d.shape[1])
        dk_p.append(k_d[:, -keep:])
        dv_p.append(v_d[:, -keep:])
        sg_p.append(s_fin.astype(x_prompt.dtype))
        mk_p.append(mem_k)
        mv_p.append(mem_v)
        q_g, k_g, v_g, r_g, log_a, q_d, k_d, v_d = project_mixer_inputs(xs, w_in[l], w_gate_lr[l], b_gate[l])
        o_g, s_new = gla_chunked(q_g, k_g, v_g, log_a, state_gla[l], xs.shape[1])
        o_d, nk, nv = dilated_attention_sample(q_d, k_d, v_d, cache_dil_k[l], cache_dil_v[l])
        mix = mixer_output(o_g, r_g, o_d, g_gla_norm[l], w_out[l])
        xs = post_mixer(xs, mix, cache_mem_k[l], cache_mem_v[l], *post)
        dk_s.append(nk)
        dv_s.append(nv)
        sg_s.append(s_new.astype(state_gla.dtype))
    new_dil_k_prompt = jnp.stack(dk_p, 0)
    new_dil_v_prompt = jnp.stack(dv_p, 0)
    new_state_gla_prompt = jnp.stack(sg_p, 0)
    new_mem_k_prompt = jnp.stack(mk_p, 0)
    new_mem_v_prompt = jnp.stack(mv_p, 0)
    new_dil_k_sample = jnp.stack(dk_s, 0)
    new_dil_v_sample = jnp.stack(dv_s, 0)
    new_state_gla_sample = jnp.stack(sg_s, 0)
    return (xp, xs, new_dil_k_prompt, new_dil_v_prompt, new_state_gla_prompt, new_mem_k_prompt, new_mem_v_prompt, new_dil_k_sample, new_dil_v_sample, new_state_gla_sample)
```

```python
import functools

import numpy as np
import jax
import jax.numpy as jnp
from jax import lax
from jax.experimental import pallas as pl
from jax.experimental.pallas import tpu as pltpu

F32 = jnp.float32
BF16 = jnp.bfloat16

D_MODEL = 1024
H_GLA = 4
GLA_DK = 64
GLA_DV = 128
GLA_KEY_W = H_GLA * GLA_DK
GLA_VAL_W = H_GLA * GLA_DV
GATE_RANK = 16
GATE_TAU = 16.0
GLA_CHUNK = 64
H_DIL = 4
DIL_DH = 128
DIL_W = H_DIL * DIL_DH
DIL_PAIRS = ((128, 1), (512, 4), (2048, 16))
DIL_SPAN = 128
WINDOW_MAX = 2048
MEM_HEADS = 4
MEM_DH = 256
N_GROUPS = 4
EXPERTS_PER_GROUP = 8
N_EXPERTS = N_GROUPS * EXPERTS_PER_GROUP
EXPERT_HIDDEN = 512
DEPTH = 1
ALPHA = (2.0 * DEPTH) ** 0.25
LN_EPS = 1e-5
NEG_INF = -1e30

LANES = 128
VMEM_LIMIT = 56 * 1024 * 1024

TOKEN_TILE = 512
MOE_TILE = 256
FINAL_TILE = 256


def _mm(a, b):
    return jnp.dot(a, b, preferred_element_type=F32)


def _mm_nt(a, b):
    return lax.dot_general(a, b, (((1,), (1,)), ((), ())), preferred_element_type=F32)


def _mm_tn(a, b):
    return lax.dot_general(a, b, (((0,), (0,)), ((), ())), preferred_element_type=F32)


def _layer_norm(v, g, b):
    mu = jnp.mean(v, axis=-1, keepdims=True)
    d = v - mu
    var = jnp.mean(d * d, axis=-1, keepdims=True)
    return d * lax.rsqrt(var + LN_EPS) * g + b


def _params(sem):
    return pltpu.CompilerParams(dimension_semantics=sem, vmem_limit_bytes=VMEM_LIMIT)


def _proj_body(x_ref, w_ref, wa_ref, wgl_ref, bg_ref,
               qg_o, kg_o, vg_o, r_o, la_o, qd_o, kd_o, vd_o):
    xb = x_ref[...].astype(BF16)

    def mm(lo, hi):
        return _mm(xb, w_ref[:, lo:hi])

    qg_o[...] = mm(0, 256) * (GLA_DK ** -0.5)
    kg_o[...] = mm(256, 512)
    vg_o[...] = mm(512, 1024)
    r_o[...] = mm(1024, 1536)
    qd_o[...] = mm(1536, 2048)
    kd_o[...] = mm(2048, 2560)
    vd_o[...] = mm(2560, 3072)
    a_lr = _mm(xb, wa_ref[...])
    z = _mm(a_lr.astype(BF16), wgl_ref[...]) + bg_ref[...]
    la_o[...] = (jnp.minimum(z, 0.0) - jnp.log1p(jnp.exp(-jnp.abs(z)))) * (1.0 / GATE_TAU)


def _proj(x, w_main, w_a, w_gl, b_g):
    t = x.shape[0]
    tm = min(TOKEN_TILE, t)
    widths = (256, 256, 512, 512, 256, 512, 512, 512)
    row = lambda i: (i, 0)
    const = lambda i: (0, 0)
    return pl.pallas_call(
        _proj_body,
        out_shape=[jax.ShapeDtypeStruct((t, w), F32) for w in widths],
        grid=(t // tm,),
        in_specs=[pl.BlockSpec((tm, D_MODEL), row),
                  pl.BlockSpec(w_main.shape, const),
                  pl.BlockSpec(w_a.shape, const),
                  pl.BlockSpec(w_gl.shape, const),
                  pl.BlockSpec(b_g.shape, const)],
        out_specs=[pl.BlockSpec((tm, w), row) for w in widths],
        compiler_params=_params(("arbitrary",)),
        name="proj",
    )(x, w_main, w_a, w_gl, b_g)


def _split3(a):
    a1 = a.astype(BF16)
    r1 = a - a1.astype(F32)
    a2 = r1.astype(BF16)
    r2 = r1 - a2.astype(F32)
    return a1, a2, r2.astype(BF16)


def _gla_body(q_ref, k_ref, v_ref, la_ref, s0_ref, o_ref, sfin_ref, s_scr, *, bb, rows, nchunk):
    c = GLA_CHUNK
    j = pl.program_id(1)

    @pl.when(j == 0)
    def _():
        s_scr[...] = s0_ref[...]

    ri = lax.broadcasted_iota(jnp.int32, (c, c), 0)
    ci = lax.broadcasted_iota(jnp.int32, (c, c), 1)
    causal = ci <= ri
    tri = jnp.where(causal, 1.0, 0.0).astype(BF16)
    ones = jnp.ones((c, GLA_DV), BF16)

    def pad(t):
        if rows == c * nchunk:
            return t
        return jnp.concatenate([t, jnp.zeros((c * nchunk - rows, t.shape[1]), t.dtype)], axis=0)

    for ib in range(bb):
        q_all = pad(q_ref[ib])
        k_all = pad(k_ref[ib])
        v_all = pad(v_ref[ib])
        la_all = pad(la_ref[ib])
        for ch in range(nchunk):
            sl = slice(ch * c, (ch + 1) * c)
            a = la_all[sl]
            p1, p2, p3 = _split3(a)
            cum = _mm(tri, jnp.concatenate([p1, p2, p3], axis=1))
            b = cum[:, :GLA_KEY_W] + cum[:, GLA_KEY_W:2 * GLA_KEY_W] + cum[:, 2 * GLA_KEY_W:]
            outs = []
            for h in range(H_GLA):
                hk = slice(h * GLA_DK, (h + 1) * GLA_DK)
                hv = slice(h * GLA_DV, (h + 1) * GLA_DV)
                bh = b[:, hk]
                qh = q_all[sl, hk]
                kh = k_all[sl, hk]
                vh = v_all[sl, hv].astype(BF16)
                b_last = bh[c - 1:c, :]
                q_t = (qh * jnp.exp(bh)).astype(BF16)
                k_t = (kh * jnp.exp(-bh)).astype(BF16)
                k_end = (kh * jnp.exp(b_last - bh)).astype(BF16)
                a_in = jnp.where(causal, _mm_nt(q_t, k_t), 0.0).astype(BF16)
                s = s_scr[ib, h]
                outs.append(_mm(a_in, vh) + _mm(q_t, s.astype(BF16)))
                pieces = jnp.concatenate([p1[:, hk], p2[:, hk], p3[:, hk]], axis=1)
                bl = _mm_tn(pieces, ones)
                bl = bl[:GLA_DK] + bl[GLA_DK:2 * GLA_DK] + bl[2 * GLA_DK:]
                s_scr[ib, h] = jnp.exp(bl) * s + _mm_tn(k_end, vh)
            o_chunk = jnp.concatenate(outs, axis=1)
            if rows == c * nchunk:
                o_ref[ib, sl, :] = o_chunk
            else:
                o_ref[ib] = o_chunk[:rows]

    @pl.when(j == pl.num_programs(1) - 1)
    def _():
        sfin_ref[...] = s_scr[...]


def _gla(q, k, v, la, s0, *, bb, nchunk):
    nb, s, _ = q.shape
    rows = min(s, GLA_CHUNK * nchunk)
    assert s % rows == 0 and nb % bb == 0
    blk = lambda i, j: (i, j, 0)
    st = lambda i, j: (i, 0, 0, 0)
    body = functools.partial(_gla_body, bb=bb, rows=rows, nchunk=nchunk)
    return pl.pallas_call(
        body,
        out_shape=[jax.ShapeDtypeStruct((nb, s, GLA_VAL_W), F32),
                   jax.ShapeDtypeStruct((nb, H_GLA, GLA_DK, GLA_DV), F32)],
        grid=(nb // bb, s // rows),
        in_specs=[pl.BlockSpec((bb, rows, GLA_KEY_W), blk),
                  pl.BlockSpec((bb, rows, GLA_KEY_W), blk),
                  pl.BlockSpec((bb, rows, GLA_VAL_W), blk),
                  pl.BlockSpec((bb, rows, GLA_KEY_W), blk),
                  pl.BlockSpec((bb, H_GLA, GLA_DK, GLA_DV), st)],
        out_specs=[pl.BlockSpec((bb, rows, GLA_VAL_W), blk),
                   pl.BlockSpec((bb, H_GLA, GLA_DK, GLA_DV), st)],
        scratch_shapes=[pltpu.VMEM((bb, H_GLA, GLA_DK, GLA_DV), F32)],
        compiler_params=_params(("arbitrary", "arbitrary")),
        name="gla",
    )(q, k, v, la, s0)


DIL_QBLK = 128


def _dil_body(*refs, nq, first, last):
    if first:
        q_ref, kc_ref, kp_ref, vc_ref, vp_ref = refs[:5]
        rest = refs[5:]
    else:
        q_ref, kc_ref, kp_ref, vc_ref, vp_ref, op_ref, lp_ref = refs[:7]
        rest = refs[7:]
    if last:
        (o_ref,) = rest
    else:
        o_ref, l_ref = rest
    n = pl.program_id(2)
    blk = DIL_QBLK
    scale = DIL_DH ** -0.5
    ri = lax.broadcasted_iota(jnp.int32, (blk, 2 * blk), 0)
    ci = lax.broadcasted_iota(jnp.int32, (blk, 2 * blk), 1)
    band = (ci >= ri) & (ci <= ri + DIL_SPAN)
    band0 = band & ((ci >= blk) | (n > 0))

    for j in range(nq):
        rows = slice(j * blk, (j + 1) * blk)
        qj = q_ref[rows, :].astype(BF16)
        if j == 0:
            kw = jnp.concatenate([kp_ref[...], kc_ref[0:blk, :]], axis=0)
            vw = jnp.concatenate([vp_ref[...], vc_ref[0:blk, :]], axis=0)
            mask = band0
        else:
            kw = kc_ref[(j - 1) * blk:(j + 1) * blk, :]
            vw = vc_ref[(j - 1) * blk:(j + 1) * blk, :]
            mask = band
        s = _mm_nt(qj, kw.astype(BF16)) * scale
        s = jnp.where(mask, s, NEG_INF)
        m = jnp.max(s, axis=-1, keepdims=True)
        p = jnp.exp(s - m)
        den = jnp.sum(p, axis=-1, keepdims=True)
        o = _mm(p.astype(BF16), vw.astype(BF16)) / den
        lse = m + jnp.log(den)
        if not first:
            lse_p = lp_ref[rows, :]
            mx = jnp.maximum(lse, lse_p)
            w_p = jnp.exp(lse_p - mx)
            w_c = jnp.exp(lse - mx)
            tot = w_p + w_c
            o = (w_p * op_ref[rows, :] + w_c * o) / tot
            lse = mx + jnp.log(tot)
        o_ref[rows, :] = o
        if not last:
            l_ref[rows, :] = jnp.broadcast_to(lse, (blk, LANES))


def _dil_pass(q, k, v, prev, dil, *, last):
    bsz, s, _ = q.shape
    length = s // dil
    width = dil * DIL_W
    r = min(512, length)
    nq = r // DIL_QBLK
    first = prev is None
    view = lambda t: t.reshape(bsz, length, width)
    cur = pl.BlockSpec((None, r, DIL_DH), lambda b, c, n: (b, n, c))
    prv = pl.BlockSpec((None, DIL_QBLK, DIL_DH), lambda b, c, n: (b, jnp.maximum(n * nq - 1, 0), c))
    args = [view(q), view(k), view(k), view(v), view(v)]
    in_specs = [cur, cur, prv, cur, prv]
    if not first:
        args += [view(prev[0]), view(prev[1])]
        in_specs += [cur, cur]
    n_out = 1 if last else 2
    outs = pl.pallas_call(
        functools.partial(_dil_body, nq=nq, first=first, last=last),
        out_shape=[jax.ShapeDtypeStruct((bsz, length, width), F32)] * n_out,
        grid=(bsz, dil * H_DIL, length // r),
        in_specs=in_specs,
        out_specs=[cur] * n_out,
        compiler_params=_params(("arbitrary", "arbitrary", "arbitrary")),
        name=f"dil_d{dil}",
    )(*args)
    return [t.reshape(bsz, s, DIL_W) for t in outs]


def _dil_prompt(q, k, v):
    prev = None
    for i, (_, dil) in enumerate(DIL_PAIRS):
        prev = _dil_pass(q, k, v, prev, dil, last=(i == len(DIL_PAIRS) - 1))
    return prev[0]


def _dil_sample_counts(buf, t_new, pad_new):
    idx = np.concatenate([np.arange(buf), buf + np.arange(pad_new)])
    real = np.concatenate([np.ones(buf, bool), np.arange(pad_new) < t_new])
    cnt = np.zeros((t_new, idx.size), np.float32)
    for window, dil in DIL_PAIRS:
        delta = buf + np.arange(t_new)[:, None] - idx[None, :]
        ok = (delta >= 0) & (delta % dil == 0) & (delta // dil <= window // dil) & real[None, :]
        cnt += ok
    return np.tile(cnt, (H_DIL, 1))


def _dil_sample_body(q_ref, kn_ref, vn_ref, ck_ref, cv_ref, cnt_ref, nk_ref, nv_ref, o_ref, *, buf, t_new):
    keep = buf - t_new
    nk_ref[0:keep, :] = ck_ref[t_new:buf, :]
    nv_ref[0:keep, :] = cv_ref[t_new:buf, :]
    k_new = kn_ref[...]
    v_new = vn_ref[...]
    nk_ref[keep:buf, :] = k_new
    nv_ref[keep:buf, :] = v_new

    q = q_ref[...]
    rows = H_DIL * t_new
    qrep = jnp.concatenate([q] * H_DIL, axis=0)
    rh = lax.broadcasted_iota(jnp.int32, (rows, DIL_W), 0) // t_new
    ch = lax.broadcasted_iota(jnp.int32, (rows, DIL_W), 1) // DIL_DH
    own = rh == ch
    qbd = jnp.where(own, qrep, 0.0).astype(BF16)
    zpad = jnp.zeros((LANES - t_new, DIL_W), F32)
    k_pad = jnp.concatenate([k_new, zpad], axis=0).astype(BF16)
    v_pad = jnp.concatenate([v_new, zpad], axis=0).astype(BF16)
    s = jnp.concatenate([_mm_nt(qbd, ck_ref[...].astype(BF16)), _mm_nt(qbd, k_pad)], axis=1)
    s = s * (DIL_DH ** -0.5)
    cnt = cnt_ref[...]
    s = jnp.where(cnt > 0.0, s, NEG_INF)
    m = jnp.max(s, axis=-1, keepdims=True)
    p = jnp.exp(s - m) * cnt
    den = jnp.sum(p, axis=-1, keepdims=True)
    pb = p.astype(BF16)
    o = (_mm(pb[:, :buf], cv_ref[...].astype(BF16)) + _mm(pb[:, buf:], v_pad)) / den
    o = jnp.where(own, o, 0.0)
    acc = o[0:t_new]
    for h in range(1, H_DIL):
        acc = acc + o[h * t_new:(h + 1) * t_new]
    o_ref[...] = acc


def _dil_sample(q, k_new, v_new, cache_k, cache_v):
    db, t_new, _ = q.shape
    buf = cache_k.shape[1]
    cnt = jnp.asarray(_dil_sample_counts(buf, t_new, LANES))
    new = pl.BlockSpec((None, t_new, DIL_W), lambda b: (b, 0, 0))
    big = pl.BlockSpec((None, buf, DIL_W), lambda b: (b, 0, 0))
    return pl.pallas_call(
        functools.partial(_dil_sample_body, buf=buf, t_new=t_new),
        out_shape=[jax.ShapeDtypeStruct((db, buf, DIL_W), F32),
                   jax.ShapeDtypeStruct((db, buf, DIL_W), F32),
                   jax.ShapeDtypeStruct((db, t_new, DIL_W), F32)],
        grid=(db,),
        in_specs=[new, new, new, big, big, pl.BlockSpec(cnt.shape, lambda b: (0, 0))],
        out_specs=[big, big, new],
        compiler_params=_params(("arbitrary",)),
        name="dil_sample",
    )(q, k_new, v_new, cache_k, cache_v, cnt)


def _post_a_body(x_ref, og_ref, r_ref, od_ref, gg_ref, wo_ref, lg_ref, lb_ref, wq_ref, x1_o, qm_o):
    og = og_ref[...]
    parts = []
    for h in range(H_GLA):
        oh = og[:, h * GLA_DV:(h + 1) * GLA_DV]
        parts.append(oh * lax.rsqrt(jnp.mean(oh * oh, axis=-1, keepdims=True) + LN_EPS))
    r = r_ref[...]
    on = jnp.concatenate(parts, axis=1) * gg_ref[...] * (r * jax.nn.sigmoid(r))
    cat = jnp.concatenate([on.astype(BF16), od_ref[...].astype(BF16)], axis=1)
    mix = _mm(cat, wo_ref[...])
    x1 = _layer_norm(ALPHA * x_ref[...] + mix, lg_ref[...], lb_ref[...])
    x1_o[...] = x1
    qm_o[...] = _mm(x1.astype(BF16), wq_ref[...]).astype(BF16)


def _post_a(x, og, r, od, gg, wo, lg, lb, wq):
    t = x.shape[0]
    tm = min(TOKEN_TILE, t)
    row = lambda i: (i, 0)
    const = lambda i: (0, 0)
    full = lambda a: pl.BlockSpec(a.shape, const)
    return pl.pallas_call(
        _post_a_body,
        out_shape=[jax.ShapeDtypeStruct((t, D_MODEL), F32), jax.ShapeDtypeStruct((t, D_MODEL), BF16)],
        grid=(t // tm,),
        in_specs=[pl.BlockSpec((tm, D_MODEL), row), pl.BlockSpec((tm, GLA_VAL_W), row),
                  pl.BlockSpec((tm, GLA_VAL_W), row), pl.BlockSpec((tm, DIL_W), row),
                  full(gg), full(wo), full(lg), full(lb), full(wq)],
        out_specs=[pl.BlockSpec((tm, D_MODEL), row), pl.BlockSpec((tm, D_MODEL), row)],
        compiler_params=_params(("arbitrary",)),
        name="post_a",
    )(x, og, r, od, gg, wo, lg, lb, wq)


def _mem_kv_body(m_ref, wk_ref, wv_ref, k_o, v_o):
    mb = m_ref[...].astype(BF16)
    k_o[...] = _mm(mb, wk_ref[...])
    v_o[...] = _mm(mb, wv_ref[...])


def _mem_kv(mem, wk, wv):
    bsz, mt, _ = mem.shape
    blk = pl.BlockSpec((None, mt, D_MODEL), lambda b: (b, 0, 0))
    w = pl.BlockSpec((D_MODEL, D_MODEL), lambda b: (0, 0))
    return pl.pallas_call(
        _mem_kv_body,
        out_shape=[jax.ShapeDtypeStruct((bsz, mt, D_MODEL), F32)] * 2,
        grid=(bsz,),
        in_specs=[blk, w, w],
        out_specs=[blk, blk],
        compiler_params=_params(("arbitrary",)),
        name="mem_kv",
    )(mem, wk, wv)


def _mem_attn_body(q_ref, k_ref, v_ref, o_ref, *, g):
    scale = MEM_DH ** -0.5
    for ig in range(g):
        outs = []
        for h in range(MEM_HEADS):
            hs = slice(h * MEM_DH, (h + 1) * MEM_DH)
            s = _mm_nt(q_ref[ig, :, hs], k_ref[ig, :, hs].astype(BF16)) * scale
            m = jnp.max(s, axis=-1, keepdims=True)
            p = jnp.exp(s - m)
            p = p / jnp.sum(p, axis=-1, keepdims=True)
            outs.append(_mm(p.astype(BF16), v_ref[ig, :, hs].astype(BF16)))
        o_ref[ig] = jnp.concatenate(outs, axis=1).astype(BF16)


def _mem_attn(q, mk, mv, *, g, kv_of_step):
    nq, tq, _ = q.shape
    mt = mk.shape[1]
    qs = pl.BlockSpec((g, tq, D_MODEL), lambda i: (i, 0, 0))
    ks = pl.BlockSpec((g, mt, D_MODEL), lambda i: (kv_of_step(i), 0, 0))
    return pl.pallas_call(
        functools.partial(_mem_attn_body, g=g),
        out_shape=jax.ShapeDtypeStruct((nq, tq, D_MODEL), BF16),
        grid=(nq // g,),
        in_specs=[qs, ks, ks],
        out_specs=qs,
        compiler_params=_params(("arbitrary",)),
        name="mem_attn",
    )(q, mk, mv)


ROUTE_GROUP_LANE0 = N_EXPERTS


def _post_c_body(x1p_ref, aop_ref, x1s_ref, aos_ref, wo_ref, lg_ref, lb_ref, wr_ref, br_ref,
                 x2_o, eid_o, ew_o, *, prompt_steps):
    def run(x1_ref, ao_ref):
        x2 = _layer_norm(ALPHA * x1_ref[...] + _mm(ao_ref[...], wo_ref[...]), lg_ref[...], lb_ref[...])
        x2_o[...] = x2
        logits = _mm(x2.astype(BF16), wr_ref[...]) + br_ref[...]
        lane = lax.broadcasted_iota(jnp.int32, logits.shape, 1).astype(F32)
        big = float(LANES)

        def first_argmax(vals, vmax):
            return jnp.min(jnp.where(vals == vmax, lane, big), axis=-1, keepdims=True)

        is_g = (lane >= ROUTE_GROUP_LANE0) & (lane < ROUTE_GROUP_LANE0 + N_GROUPS)
        gl = jnp.where(is_g, logits, NEG_INF)
        gmax = jnp.max(gl, axis=-1, keepdims=True)
        g_sel = first_argmax(gl, gmax) - ROUTE_GROUP_LANE0
        g_w = 1.0 / jnp.sum(jnp.where(is_g, jnp.exp(gl - gmax), 0.0), axis=-1, keepdims=True)
        lo = g_sel * EXPERTS_PER_GROUP
        el = jnp.where((lane >= lo) & (lane < lo + EXPERTS_PER_GROUP), logits, NEG_INF)
        v1 = jnp.max(el, axis=-1, keepdims=True)
        i1 = first_argmax(el, v1)
        el2 = jnp.where(lane == i1, NEG_INF, el)
        v2 = jnp.max(el2, axis=-1, keepdims=True)
        i2 = first_argmax(el2, v2)
        e = jnp.exp(v2 - v1)
        w1 = g_w / (1.0 + e)
        w2 = g_w * e / (1.0 + e)
        eid_o[...] = jnp.where(lane == 0.0, i1, jnp.where(lane == 1.0, i2, 0.0)).astype(jnp.int32)
        ew_o[...] = jnp.where(lane == 0.0, w1, jnp.where(lane == 1.0, w2, 0.0))

    i = pl.program_id(0)

    @pl.when(i < prompt_steps)
    def _():
        run(x1p_ref, aop_ref)

    @pl.when(i >= prompt_steps)
    def _():
        run(x1s_ref, aos_ref)


def _post_c(x1p, aop, x1s, aos, wo, lg, lb, wr, br):
    n_p, n_s = x1p.shape[0], x1s.shape[0]
    tm = int(np.gcd(np.gcd(n_p, n_s), TOKEN_TILE))
    sp, ss = n_p // tm, n_s // tm
    prow = lambda i: (jnp.minimum(i, sp - 1), 0)
    srow = lambda i: (jnp.maximum(i - sp, 0), 0)
    orow = lambda i: (i, 0)
    const = lambda i: (0, 0)
    full = lambda a: pl.BlockSpec(a.shape, const)
    n_total = n_p + n_s
    return pl.pallas_call(
        functools.partial(_post_c_body, prompt_steps=sp),
        out_shape=[jax.ShapeDtypeStruct((n_total, D_MODEL), F32),
                   jax.ShapeDtypeStruct((n_total, LANES), jnp.int32),
                   jax.ShapeDtypeStruct((n_total, LANES), F32)],
        grid=(sp + ss,),
        in_specs=[pl.BlockSpec((tm, D_MODEL), prow), pl.BlockSpec((tm, D_MODEL), prow),
                  pl.BlockSpec((tm, D_MODEL), srow), pl.BlockSpec((tm, D_MODEL), srow),
                  full(wo), full(lg), full(lb), full(wr), full(br)],
        out_specs=[pl.BlockSpec((tm, D_MODEL), orow), pl.BlockSpec((tm, LANES), orow),
                   pl.BlockSpec((tm, LANES), orow)],
        compiler_params=_params(("arbitrary",)),
        name="post_c",
    )(x1p, aop, x1s, aos, wo, lg, lb, wr, br)


def _moe_body(te_ref, tv_ref, idx0_ref, idxn_ref, x_hbm, wg_ref, wu_ref, wd_ref, o_ref,
              xbuf, idx_smem, gsem, isem, wgb, wub, wdb, *, nt):
    t = pl.program_id(0)
    tile = MOE_TILE

    def issue(idx_vmem, slot):
        cp = pltpu.make_async_copy(idx_vmem.at[0], idx_smem, isem)
        cp.start()
        cp.wait()

        def body(i, carry):
            tok = idx_smem[0, i]
            pltpu.make_async_copy(x_hbm.at[pl.ds(tok, 1)], xbuf.at[slot, pl.ds(i, 1)], gsem.at[slot]).start()
            return carry

        lax.fori_loop(0, tile, body, 0, unroll=8)

    @pl.when(t == 0)
    def _():
        issue(idx0_ref, 0)

    @pl.when((t + 1 < nt) & (tv_ref[jnp.minimum(t + 1, nt - 1)] > 0))
    def _():
        issue(idxn_ref, (t + 1) % 2)

    @pl.when((t == 0) | (te_ref[t] != te_ref[jnp.maximum(t - 1, 0)]))
    def _():
        wgb[...] = wg_ref[...].astype(BF16)
        wub[...] = wu_ref[...].astype(BF16)
        wdb[...] = wd_ref[...].astype(BF16)

    @pl.when(tv_ref[t] > 0)
    def _():
        slot = t % 2
        pltpu.make_async_copy(x_hbm.at[pl.ds(0, tile)], xbuf.at[slot], gsem.at[slot]).wait()
        xb = xbuf[slot].astype(BF16)
        hg = _mm(xb, wgb[...])
        hu = _mm(xb, wub[...])
        h = (hg * jax.nn.sigmoid(hg) * hu).astype(BF16)
        o_ref[...] = _mm(h, wdb[...])

    @pl.when(tv_ref[t] == 0)
    def _():
        o_ref[...] = jnp.zeros_like(o_ref)


def _moe(tile_expert, tile_valid, row_src, x2, wg, wu, wd):
    nt = tile_expert.shape[0]
    tile = MOE_TILE
    idx = row_src.reshape(nt, 1, tile)
    wspec = lambda shape: pl.BlockSpec((None,) + shape, lambda t, te, tv: (te[t], 0, 0))
    grid_spec = pltpu.PrefetchScalarGridSpec(
        num_scalar_prefetch=2,
        grid=(nt,),
        in_specs=[pl.BlockSpec((1, 1, tile), lambda t, te, tv: (0, 0, 0)),
                  pl.BlockSpec((1, 1, tile), lambda t, te, tv: (jnp.minimum(t + 1, nt - 1), 0, 0)),
                  pl.BlockSpec(memory_space=pl.ANY),
                  wspec((D_MODEL, EXPERT_HIDDEN)), wspec((D_MODEL, EXPERT_HIDDEN)),
                  wspec((EXPERT_HIDDEN, D_MODEL))],
        out_specs=pl.BlockSpec((tile, D_MODEL), lambda t, te, tv: (t, 0)),
        scratch_shapes=[pltpu.VMEM((2, tile, D_MODEL), F32),
                        pltpu.SMEM((1, tile), jnp.int32),
                        pltpu.SemaphoreType.DMA((2,)),
                        pltpu.SemaphoreType.DMA,
                        pltpu.VMEM((D_MODEL, EXPERT_HIDDEN), BF16),
                        pltpu.VMEM((D_MODEL, EXPERT_HIDDEN), BF16),
                        pltpu.VMEM((EXPERT_HIDDEN, D_MODEL), BF16)])
    return pl.pallas_call(
        functools.partial(_moe_body, nt=nt),
        out_shape=jax.ShapeDtypeStruct((nt * tile, D_MODEL), F32),
        grid_spec=grid_spec,
        compiler_params=_params(("arbitrary",)),
        name="moe",
    )(tile_expert, tile_valid, idx, idx, x2, wg, wu, wd)


def _final_body(idx0_ref, idxn_ref, x_ref, ew_ref, lg_ref, lb_ref, eo_hbm, y_ref,
                obuf, idx_smem, gsem, isem, *, nt):
    t = pl.program_id(0)
    tile = FINAL_TILE

    def issue(idx_vmem, slot):
        cp = pltpu.make_async_copy(idx_vmem.at[0], idx_smem, isem)
        cp.start()
        cp.wait()

        def body(i, carry):
            row = idx_smem[0, i]
            pltpu.make_async_copy(eo_hbm.at[pl.ds(row, 1)], obuf.at[slot, pl.ds(i, 1)], gsem.at[slot]).start()
            return carry

        lax.fori_loop(0, 2 * tile, body, 0, unroll=8)

    @pl.when(t == 0)
    def _():
        issue(idx0_ref, 0)

    @pl.when(t + 1 < nt)
    def _():
        issue(idxn_ref, (t + 1) % 2)

    slot = t % 2
    pltpu.make_async_copy(eo_hbm.at[pl.ds(0, 2 * tile)], obuf.at[slot], gsem.at[slot]).wait()
    ew = ew_ref[...]
    moe = ew[:, 0:1] * obuf[slot, 0:tile, :] + ew[:, 1:2] * obuf[slot, tile:2 * tile, :]
    y_ref[...] = _layer_norm(ALPHA * x_ref[...] + moe, lg_ref[...], lb_ref[...])


def _final(dest, x2, ew, lg, lb, eo, row0, t):
    tile = FINAL_TILE
    nt = t // tile
    b0 = row0 // tile
    const = lambda i: (0, 0)
    return pl.pallas_call(
        functools.partial(_final_body, nt=nt),
        out_shape=jax.ShapeDtypeStruct((t, D_MODEL), F32),
        grid=(nt,),
        in_specs=[pl.BlockSpec((1, 1, 2 * tile), lambda i: (b0, 0, 0)),
                  pl.BlockSpec((1, 1, 2 * tile), lambda i: (b0 + jnp.minimum(i + 1, nt - 1), 0, 0)),
                  pl.BlockSpec((tile, D_MODEL), lambda i: (b0 + i, 0)),
                  pl.BlockSpec((tile, LANES), lambda i: (b0 + i, 0)),
                  pl.BlockSpec(lg.shape, const), pl.BlockSpec(lb.shape, const),
                  pl.BlockSpec(memory_space=pl.ANY)],
        out_specs=pl.BlockSpec((tile, D_MODEL), lambda i: (i, 0)),
        scratch_shapes=[pltpu.VMEM((2, 2 * tile, D_MODEL), F32),
                        pltpu.SMEM((1, 2 * tile), jnp.int32),
                        pltpu.SemaphoreType.DMA((2,)),
                        pltpu.SemaphoreType.DMA],
        compiler_params=_params(("arbitrary",)),
        name="final",
    )(dest, dest, x2, ew, lg, lb, eo)


def _routing_tables(eid, n_tok):
    tile = MOE_TILE
    n_assign = 2 * n_tok
    nt = -(-n_assign // tile) + N_EXPERTS
    flat = eid.reshape(-1)
    order = jnp.argsort(flat, stable=True).astype(jnp.int32)
    e_sorted = flat[order]
    counts = jnp.sum((flat[:, None] == jnp.arange(N_EXPERTS, dtype=jnp.int32)[None, :]).astype(jnp.int32), axis=0)
    padded = ((counts + tile - 1) // tile) * tile
    g_end = jnp.cumsum(padded)
    g_start = g_end - padded
    u_start = jnp.cumsum(counts) - counts
    tile_start = jnp.arange(nt, dtype=jnp.int32) * tile
    tile_expert = jnp.minimum(jnp.searchsorted(g_end, tile_start, side="right"), N_EXPERTS - 1).astype(jnp.int32)
    tile_valid = (tile_start < g_end[-1]).astype(jnp.int32)
    p = jnp.arange(nt * tile, dtype=jnp.int32)
    e_p = tile_expert[p // tile]
    off = p - g_start[e_p]
    ok = (off < counts[e_p]) & (tile_valid[p // tile] > 0)
    src = order[jnp.clip(u_start[e_p] + off, 0, n_assign - 1)] // 2
    row_src = jnp.where(ok, src, 0).astype(jnp.int32)
    dest_sorted = g_start[e_sorted] + jnp.arange(n_assign, dtype=jnp.int32) - u_start[e_sorted]
    dest = jnp.zeros((n_assign,), jnp.int32).at[order].set(dest_sorted.astype(jnp.int32))
    dest = dest.reshape(n_tok // FINAL_TILE, FINAL_TILE, 2).transpose(0, 2, 1).reshape(n_tok // FINAL_TILE, 1, 2 * FINAL_TILE)
    return tile_expert, tile_valid, row_src, dest


def kernel(x_prompt, x_sample, mem_prompt, cache_dil_k, cache_dil_v, state_gla, cache_mem_k, cache_mem_v, w_in, w_gate_lr, b_gate, g_gla_norm, w_out, ln_mix_g, ln_mix_b, w_mem_q, w_mem_k, w_mem_v, w_mem_o, ln_mem_g, ln_mem_b, w_route_group, b_route_group, w_route_expert, b_route_expert, w_exp_gate, w_exp_up, w_exp_down, ln_ffn_g, ln_ffn_b):
    assert w_in.shape[0] == DEPTH
    bp, seq, _ = x_prompt.shape
    db, tdec, _ = x_sample.shape
    buf = cache_dil_k.shape[2]
    mt = mem_prompt.shape[1]
    n_p, n_s = bp * seq, db * tdec
    n_tok = n_p + n_s
    l = 0

    a0 = 2 * GLA_KEY_W + 2 * GLA_VAL_W
    w_main = jnp.concatenate([w_in[l][:, :a0], w_in[l][:, a0 + GATE_RANK:]], axis=1).astype(BF16)
    w_a = jnp.pad(w_in[l][:, a0:a0 + GATE_RANK], ((0, 0), (0, LANES - GATE_RANK))).astype(BF16)
    w_gl = jnp.pad(w_gate_lr[l], ((0, LANES - GATE_RANK), (0, 0))).astype(BF16)
    b_g = b_gate[l][None, :]
    row1 = lambda a: a[l][None, :]
    w_o = w_out[l].astype(BF16)
    w_q = w_mem_q[l].astype(BF16)
    w_k = w_mem_k[l].astype(BF16)
    w_v = w_mem_v[l].astype(BF16)
    w_mo = w_mem_o[l].astype(BF16)
    w_r = jnp.pad(jnp.concatenate([w_route_expert[l], w_route_group[l]], axis=1),
                  ((0, 0), (0, LANES - N_EXPERTS - N_GROUPS))).astype(BF16)
    b_r = jnp.pad(jnp.concatenate([b_route_expert[l], b_route_group[l]]), (0, LANES - N_EXPERTS - N_GROUPS))[None, :]

    def mixer_inputs(x2d):
        return _proj(x2d, w_main, w_a, w_gl, b_g)

    def post_mixer(x2d, o_g, r_g, o_d, q_shape, mk, mv, g, kv_of_step):
        x1, qm = _post_a(x2d, o_g, r_g, o_d, row1(g_gla_norm), w_o, row1(ln_mix_g), row1(ln_mix_b), w_q)
        ao = _mem_attn(qm.reshape(q_shape), mk, mv, g=g, kv_of_step=kv_of_step)
        return x1, ao.reshape(-1, D_MODEL)

    xp = x_prompt.reshape(n_p, D_MODEL)
    qg, kg, vg, rg, la, qd, kd, vd = mixer_inputs(xp)
    sh = lambda t: t.reshape(bp, seq, t.shape[-1])
    s0 = jnp.zeros((bp, H_GLA, GLA_DK, GLA_DV), F32)
    o_g, s_fin_p = _gla(sh(qg), sh(kg), sh(vg), sh(la), s0, bb=1, nchunk=4)
    o_d = _dil_prompt(sh(qd), sh(kd), sh(vd))
    mem_k, mem_v = _mem_kv(mem_prompt, w_k, w_v)
    tq = min(TOKEN_TILE, seq)
    steps_per_b = seq // tq
    x1_p, ao_p = post_mixer(xp, o_g.reshape(n_p, -1), rg, o_d.reshape(n_p, -1), (n_p // tq, tq, D_MODEL),
                            mem_k, mem_v, 1, lambda i: i // steps_per_b)
    keep = min(WINDOW_MAX, seq)
    dk_p = sh(kd)[:, seq - keep:].reshape(1, bp, keep, H_DIL, DIL_DH)
    dv_p = sh(vd)[:, seq - keep:].reshape(1, bp, keep, H_DIL, DIL_DH)

    xs = x_sample.reshape(n_s, D_MODEL)
    qg, kg, vg, rg, la, qd, kd, vd = mixer_inputs(xs)
    shs = lambda t: t.reshape(db, tdec, t.shape[-1])
    gb = 8 if db % 8 == 0 else 1
    o_g, s_new = _gla(shs(qg), shs(kg), shs(vg), shs(la), state_gla[l], bb=gb, nchunk=1)
    nk, nv, o_d = _dil_sample(shs(qd), shs(kd), shs(vd),
                              cache_dil_k[l].reshape(db, buf, DIL_W), cache_dil_v[l].reshape(db, buf, DIL_W))
    gm = 4 if db % 4 == 0 else 1
    x1_s, ao_s = post_mixer(xs, o_g.reshape(n_s, -1), rg, o_d.reshape(n_s, -1), (db, tdec, D_MODEL),
                            cache_mem_k[l].reshape(db, mt, D_MODEL), cache_mem_v[l].reshape(db, mt, D_MODEL),
                            gm, lambda i: i)

    x2, eid, ew = _post_c(x1_p, ao_p, x1_s, ao_s, w_mo, row1(ln_mem_g), row1(ln_mem_b), w_r, b_r)
    tile_expert, tile_valid, row_src, dest = _routing_tables(eid[:, :2], n_tok)
    eo = _moe(tile_expert, tile_valid, row_src, x2, w_exp_gate[l], w_exp_up[l], w_exp_down[l])
    y_p = _final(dest, x2, ew, row1(ln_ffn_g), row1(ln_ffn_b), eo, 0, n_p)
    y_s = _final(dest, x2, ew, row1(ln_ffn_g), row1(ln_ffn_b), eo, n_p, n_s)

    return (y_p.reshape(bp, seq, D_MODEL), y_s.reshape(db, tdec, D_MODEL),
            dk_p, dv_p, s_fin_p[None], mem_k.reshape(1, bp, mt, MEM_HEADS, MEM_DH),
            mem_v.reshape(1, bp, mt, MEM_HEADS, MEM_DH),
            nk.reshape(1, db, buf, H_DIL, DIL_DH), nv.reshape(1, db, buf, H_DIL, DIL_DH), s_new[None])
```

```python
import functools

import numpy as np
import jax
import jax.numpy as jnp
from jax import lax
from jax.experimental import pallas as pl
from jax.experimental.pallas import tpu as pltpu

F32 = jnp.float32
BF16 = jnp.bfloat16

D_MODEL = 1024
H_GLA = 4
GLA_DK = 64
GLA_DV = 128
GLA_KEY_W = H_GLA * GLA_DK
GLA_VAL_W = H_GLA * GLA_DV
GATE_RANK = 16
GATE_TAU = 16.0
GLA_CHUNK = 64
H_DIL = 4
DIL_DH = 128
DIL_W = H_DIL * DIL_DH
DIL_PAIRS = ((128, 1), (512, 4), (2048, 16))
DIL_SPAN = 128
WINDOW_MAX = 2048
MEM_HEADS = 4
MEM_DH = 256
N_GROUPS = 4
EXPERTS_PER_GROUP = 8
N_EXPERTS = N_GROUPS * EXPERTS_PER_GROUP
EXPERT_HIDDEN = 512
DEPTH = 1
ALPHA = (2.0 * DEPTH) ** 0.25
LN_EPS = 1e-5
NEG_INF = -1e30

LANES = 128
VMEM_LIMIT = 56 * 1024 * 1024

TOKEN_TILE = 512
MOE_TILE = 256


def _mm(a, b):
    return jnp.dot(a, b, preferred_element_type=F32)


def _mm_nt(a, b):
    return lax.dot_general(a, b, (((1,), (1,)), ((), ())), preferred_element_type=F32)


def _mm_tn(a, b):
    return lax.dot_general(a, b, (((0,), (0,)), ((), ())), preferred_element_type=F32)


def _layer_norm(v, g, b):
    mu = jnp.mean(v, axis=-1, keepdims=True)
    d = v - mu
    var = jnp.mean(d * d, axis=-1, keepdims=True)
    return d * lax.rsqrt(var + LN_EPS) * g + b


def _params(sem):
    return pltpu.CompilerParams(dimension_semantics=sem, vmem_limit_bytes=VMEM_LIMIT)


def _proj_body(x_ref, w_ref, wa_ref, wgl_ref, bg_ref,
               qg_o, kg_o, vg_o, r_o, la_o, qd_o, kd_o, vd_o):
    xb = x_ref[...].astype(BF16)

    def mm(lo, hi):
        return _mm(xb, w_ref[:, lo:hi])

    qg_o[...] = mm(0, 256) * (GLA_DK ** -0.5)
    kg_o[...] = mm(256, 512)
    vg_o[...] = mm(512, 1024)
    r_o[...] = mm(1024, 1536)
    qd_o[...] = mm(1536, 2048)
    kd_o[...] = mm(2048, 2560)
    vd_o[...] = mm(2560, 3072)
    a_lr = _mm(xb, wa_ref[...])
    z = _mm(a_lr.astype(BF16), wgl_ref[...]) + bg_ref[...]
    la_o[...] = (jnp.minimum(z, 0.0) - jnp.log1p(jnp.exp(-jnp.abs(z)))) * (1.0 / GATE_TAU)


def _proj(x, w_main, w_a, w_gl, b_g):
    t = x.shape[0]
    tm = min(TOKEN_TILE, t)
    widths = (256, 256, 512, 512, 256, 512, 512, 512)
    row = lambda i: (i, 0)
    const = lambda i: (0, 0)
    return pl.pallas_call(
        _proj_body,
        out_shape=[jax.ShapeDtypeStruct((t, w), F32) for w in widths],
        grid=(t // tm,),
        in_specs=[pl.BlockSpec((tm, D_MODEL), row),
                  pl.BlockSpec(w_main.shape, const),
                  pl.BlockSpec(w_a.shape, const),
                  pl.BlockSpec(w_gl.shape, const),
                  pl.BlockSpec(b_g.shape, const)],
        out_specs=[pl.BlockSpec((tm, w), row) for w in widths],
        compiler_params=_params(("arbitrary",)),
        name="proj",
    )(x, w_main, w_a, w_gl, b_g)


def _split3(a):
    a1 = a.astype(BF16)
    r1 = a - a1.astype(F32)
    a2 = r1.astype(BF16)
    r2 = r1 - a2.astype(F32)
    return a1, a2, r2.astype(BF16)


def _gla_body(q_ref, k_ref, v_ref, la_ref, s0_ref, o_ref, sfin_ref, s_scr, *, bb, rows, nchunk):
    c = GLA_CHUNK
    j = pl.program_id(1)

    @pl.when(j == 0)
    def _():
        s_scr[...] = s0_ref[...]

    ri = lax.broadcasted_iota(jnp.int32, (c, c), 0)
    ci = lax.broadcasted_iota(jnp.int32, (c, c), 1)
    causal = ci <= ri
    tri = jnp.where(causal, 1.0, 0.0).astype(BF16)
    ones = jnp.ones((c, GLA_DV), BF16)

    def pad(t):
        if rows == c * nchunk:
            return t
        return jnp.concatenate([t, jnp.zeros((c * nchunk - rows, t.shape[1]), t.dtype)], axis=0)

    for ib in range(bb):
        q_all = pad(q_ref[ib])
        k_all = pad(k_ref[ib])
        v_all = pad(v_ref[ib])
        la_all = pad(la_ref[ib])
        for ch in range(nchunk):
            sl = slice(ch * c, (ch + 1) * c)
            a = la_all[sl]
            p1, p2, p3 = _split3(a)
            cum = _mm(tri, jnp.concatenate([p1, p2, p3], axis=1))
            b = cum[:, :GLA_KEY_W] + cum[:, GLA_KEY_W:2 * GLA_KEY_W] + cum[:, 2 * GLA_KEY_W:]
            outs = []
            for h in range(H_GLA):
                hk = slice(h * GLA_DK, (h + 1) * GLA_DK)
                hv = slice(h * GLA_DV, (h + 1) * GLA_DV)
                bh = b[:, hk]
                qh = q_all[sl, hk]
                kh = k_all[sl, hk]
                vh = v_all[sl, hv].astype(BF16)
                b_last = bh[c - 1:c, :]
                q_t = (qh * jnp.exp(bh)).astype(BF16)
                k_t = (kh * jnp.exp(-bh)).astype(BF16)
                k_end = (kh * jnp.exp(b_last - bh)).astype(BF16)
                a_in = jnp.where(causal, _mm_nt(q_t, k_t), 0.0).astype(BF16)
                s = s_scr[ib, h]
                outs.append(_mm(a_in, vh) + _mm(q_t, s.astype(BF16)))
                pieces = jnp.concatenate([p1[:, hk], p2[:, hk], p3[:, hk]], axis=1)
                bl = _mm_tn(pieces, ones)
                bl = bl[:GLA_DK] + bl[GLA_DK:2 * GLA_DK] + bl[2 * GLA_DK:]
                s_scr[ib, h] = jnp.exp(bl) * s + _mm_tn(k_end, vh)
            o_chunk = jnp.concatenate(outs, axis=1)
            if rows == c * nchunk:
                o_ref[ib, sl, :] = o_chunk
            else:
                o_ref[ib] = o_chunk[:rows]

    @pl.when(j == pl.num_programs(1) - 1)
    def _():
        sfin_ref[...] = s_scr[...]


def _gla(q, k, v, la, s0, *, bb, nchunk):
    nb, s, _ = q.shape
    rows = min(s, GLA_CHUNK * nchunk)
    assert s % rows == 0 and nb % bb == 0
    blk = lambda i, j: (i, j, 0)
    st = lambda i, j: (i, 0, 0, 0)
    body = functools.partial(_gla_body, bb=bb, rows=rows, nchunk=nchunk)
    return pl.pallas_call(
        body,
        out_shape=[jax.ShapeDtypeStruct((nb, s, GLA_VAL_W), F32),
                   jax.ShapeDtypeStruct((nb, H_GLA, GLA_DK, GLA_DV), F32)],
        grid=(nb // bb, s // rows),
        in_specs=[pl.BlockSpec((bb, rows, GLA_KEY_W), blk),
                  pl.BlockSpec((bb, rows, GLA_KEY_W), blk),
                  pl.BlockSpec((bb, rows, GLA_VAL_W), blk),
                  pl.BlockSpec((bb, rows, GLA_KEY_W), blk),
                  pl.BlockSpec((bb, H_GLA, GLA_DK, GLA_DV), st)],
        out_specs=[pl.BlockSpec((bb, rows, GLA_VAL_W), blk),
                   pl.BlockSpec((bb, H_GLA, GLA_DK, GLA_DV), st)],
        scratch_shapes=[pltpu.VMEM((bb, H_GLA, GLA_DK, GLA_DV), F32)],
        compiler_params=_params(("arbitrary", "arbitrary")),
        name="gla",
    )(q, k, v, la, s0)


DIL_QBLK = 128
DIL_TILE = 2048
DIL_UNROLL = 4


def _dil_body(q_ref, k_ref, v_ref, o_ref, acc_ref, m_ref, l_ref, *, tile):
    blk = DIL_QBLK
    t0 = pl.program_id(2) * tile
    scale = DIL_DH ** -0.5
    ri = lax.broadcasted_iota(jnp.int32, (blk, 2 * blk), 0)
    ci = lax.broadcasted_iota(jnp.int32, (blk, 2 * blk), 1)
    band = (ci >= ri) & (ci <= ri + DIL_SPAN)

    def rows(start, dil):
        return pl.ds(start, blk) if dil == 1 else pl.ds(start, blk, stride=dil)

    def block(dil, q0, first, last):
        qr = rows(q0, dil)
        cur0 = t0 + q0
        prev0 = cur0 - dil * blk
        has_prev = prev0 >= 0
        prev0 = jnp.maximum(prev0, 0)
        kw = jnp.concatenate([k_ref[rows(prev0, dil), :], k_ref[rows(cur0, dil), :]], axis=0)
        vw = jnp.concatenate([v_ref[rows(prev0, dil), :], v_ref[rows(cur0, dil), :]], axis=0)
        s = _mm_nt(q_ref[qr, :].astype(BF16), kw.astype(BF16)) * scale
        s = jnp.where(band & ((ci >= blk) | has_prev), s, NEG_INF)
        m_new = jnp.max(s, axis=-1, keepdims=True)
        p = jnp.exp(s - m_new)
        l_new = jnp.sum(p, axis=-1, keepdims=True)
        acc = _mm(p.astype(BF16), vw.astype(BF16))
        if not first:
            m_blk, m_old = m_new, m_ref[qr, :][:, 0:1]
            m_new = jnp.maximum(m_old, m_blk)
            a_old = jnp.exp(m_old - m_new)
            a_blk = jnp.exp(m_blk - m_new)
            l_new = a_old * l_ref[qr, :][:, 0:1] + a_blk * l_new
            acc = a_old * acc_ref[qr, :] + a_blk * acc
        if last:
            o_ref[qr, :] = acc / l_new
        else:
            acc_ref[qr, :] = acc
            m_ref[qr, :] = jnp.broadcast_to(m_new, (blk, LANES))
            l_ref[qr, :] = jnp.broadcast_to(l_new, (blk, LANES))

    dils = sorted((d for _, d in DIL_PAIRS), reverse=True)
    for bi, dil in enumerate(dils):
        per_res = tile // (dil * blk)

        def body(i, carry, dil=dil, per_res=per_res, bi=bi):
            q0 = (i % dil) + (i // dil) * (dil * blk) if dil > 1 else pl.multiple_of(i * blk, blk)
            block(dil, q0, bi == 0, bi == len(dils) - 1)
            return carry

        lax.fori_loop(0, dil * per_res, body, 0, unroll=DIL_UNROLL)


def _dil_prompt(q, k, v):
    bsz, s, _ = q.shape
    tile = min(DIL_TILE, s)
    assert s % tile == 0 and all(tile % (d * DIL_QBLK) == 0 for _, d in DIL_PAIRS)
    assert all(w // d == DIL_SPAN for w, d in DIL_PAIRS)
    qs = pl.BlockSpec((None, tile, DIL_DH), lambda b, h, n: (b, n, h))
    ks = pl.BlockSpec((None, s, DIL_DH), lambda b, h, n: (b, 0, h))
    return pl.pallas_call(
        functools.partial(_dil_body, tile=tile),
        out_shape=jax.ShapeDtypeStruct((bsz, s, DIL_W), F32),
        grid=(bsz, H_DIL, s // tile),
        in_specs=[qs, ks, ks],
        out_specs=qs,
        scratch_shapes=[pltpu.VMEM((tile, DIL_DH), F32), pltpu.VMEM((tile, LANES), F32),
                        pltpu.VMEM((tile, LANES), F32)],
        compiler_params=_params(("arbitrary", "arbitrary", "arbitrary")),
        name="dil_prompt",
    )(q, k, v)


DIL_GROUP = 16
DIL_TAIL = 512


def _dil_sample_counts(buf, t_new):
    ga = (buf - DIL_TAIL) // DIL_GROUP
    pos_all = np.arange(buf)
    sel = (pos_all >= ga * DIL_GROUP) | (pos_all % DIL_GROUP < t_new)
    t = np.arange(t_new)

    def count(pos, real):
        cnt = np.zeros((t_new, pos.size), np.float32)
        for window, dil in DIL_PAIRS:
            delta = buf + t[:, None] - pos[None, :]
            cnt += (delta >= 0) & (delta % dil == 0) & (delta // dil <= window // dil) & real[None, :]
        return cnt

    assert not count(pos_all[~sel], np.ones((~sel).sum(), bool)).any()
    pad_new = LANES // H_DIL
    pos = np.concatenate([pos_all[sel], buf + np.arange(pad_new)])
    real = np.concatenate([np.ones(sel.sum(), bool), np.arange(pad_new) < t_new])
    cnt = count(pos, real)
    same_head = np.eye(H_DIL, dtype=np.float32)
    full = cnt[:, None, :, None] * same_head[None, :, None, :]
    return full.reshape(t_new * H_DIL, pos.size * H_DIL)


def _dil_sample_body(q_ref, kn_ref, vn_ref, ck_ref, cv_ref, cnt_ref, nk_ref, nv_ref, o_ref, *, ga):
    g = ck_ref.shape[0]
    half = ck_ref.shape[1] // 2
    k_new = kn_ref[...]
    v_new = vn_ref[...]
    for c_ref, n_ref, new in ((ck_ref, nk_ref, k_new), (cv_ref, nv_ref, v_new)):
        n_ref[:, 0:half, :] = c_ref[:, half:, :]
        n_ref[0:g - 1, half:, :] = c_ref[1:g, 0:half, :]
        n_ref[g - 1, half:, :] = new

    zpad = jnp.zeros((LANES - half, DIL_DH), F32)

    def keys(c_ref, new):
        main = c_ref[0:ga, 0:half, :].reshape(ga * half, DIL_DH)
        tail = c_ref[ga:g, :, :].reshape((g - ga) * 2 * half, DIL_DH)
        return jnp.concatenate([main, tail, new, zpad], axis=0).astype(BF16)

    s = _mm_nt(q_ref[...].astype(BF16), keys(ck_ref, k_new)) * (DIL_DH ** -0.5)
    cnt = cnt_ref[...]
    s = jnp.where(cnt > 0.0, s, NEG_INF)
    m = jnp.max(s, axis=-1, keepdims=True)
    p = jnp.exp(s - m) * cnt
    den = jnp.sum(p, axis=-1, keepdims=True)
    o_ref[...] = _mm(p.astype(BF16), keys(cv_ref, v_new)) / den


def _dil_sample(q, k_new, v_new, cache_k, cache_v):
    db, rows_new, _ = q.shape
    g, grows = cache_k.shape[1], cache_k.shape[2]
    assert grows == DIL_GROUP * H_DIL and rows_new * 2 == grows
    buf = g * DIL_GROUP
    ga = (buf - DIL_TAIL) // DIL_GROUP
    cnt = jnp.asarray(_dil_sample_counts(buf, rows_new // H_DIL))
    new = pl.BlockSpec((None, rows_new, DIL_DH), lambda b: (b, 0, 0))
    big = pl.BlockSpec((None, g, grows, DIL_DH), lambda b: (b, 0, 0, 0))
    return pl.pallas_call(
        functools.partial(_dil_sample_body, ga=ga),
        out_shape=[jax.ShapeDtypeStruct(cache_k.shape, F32),
                   jax.ShapeDtypeStruct(cache_v.shape, F32),
                   jax.ShapeDtypeStruct((db, rows_new, DIL_DH), F32)],
        grid=(db,),
        in_specs=[new, new, new, big, big, pl.BlockSpec(cnt.shape, lambda b: (0, 0))],
        out_specs=[big, big, new],
        compiler_params=_params(("arbitrary",)),
        name="dil_sample",
    )(q, k_new, v_new, cache_k, cache_v, cnt)


def _post_a_body(x_ref, og_ref, r_ref, od_ref, gg_ref, wo_ref, lg_ref, lb_ref, wq_ref, x1_o, qm_o):
    og = og_ref[...]
    parts = []
    for h in range(H_GLA):
        oh = og[:, h * GLA_DV:(h + 1) * GLA_DV]
        parts.append(oh * lax.rsqrt(jnp.mean(oh * oh, axis=-1, keepdims=True) + LN_EPS))
    r = r_ref[...]
    on = jnp.concatenate(parts, axis=1) * gg_ref[...] * (r * jax.nn.sigmoid(r))
    cat = jnp.concatenate([on.astype(BF16), od_ref[...].astype(BF16)], axis=1)
    mix = _mm(cat, wo_ref[...])
    x1 = _layer_norm(ALPHA * x_ref[...] + mix, lg_ref[...], lb_ref[...])
    x1_o[...] = x1
    qm_o[...] = _mm(x1.astype(BF16), wq_ref[...]).astype(BF16)


def _post_a(x, og, r, od, gg, wo, lg, lb, wq):
    t = x.shape[0]
    tm = min(TOKEN_TILE, t)
    row = lambda i: (i, 0)
    const = lambda i: (0, 0)
    full = lambda a: pl.BlockSpec(a.shape, const)
    return pl.pallas_call(
        _post_a_body,
        out_shape=[jax.ShapeDtypeStruct((t, D_MODEL), F32), jax.ShapeDtypeStruct((t, D_MODEL), BF16)],
        grid=(t // tm,),
        in_specs=[pl.BlockSpec((tm, D_MODEL), row), pl.BlockSpec((tm, GLA_VAL_W), row),
                  pl.BlockSpec((tm, GLA_VAL_W), row), pl.BlockSpec((tm, DIL_W), row),
                  full(gg), full(wo), full(lg), full(lb), full(wq)],
        out_specs=[pl.BlockSpec((tm, D_MODEL), row), pl.BlockSpec((tm, D_MODEL), row)],
        compiler_params=_params(("arbitrary",)),
        name="post_a",
    )(x, og, r, od, gg, wo, lg, lb, wq)


def _mem_kv_body(m_ref, wk_ref, wv_ref, k_o, v_o):
    mb = m_ref[...].astype(BF16)
    k_o[...] = _mm(mb, wk_ref[...])
    v_o[...] = _mm(mb, wv_ref[...])


def _mem_kv(mem, wk, wv):
    bsz, mt, _ = mem.shape
    blk = pl.BlockSpec((None, mt, D_MODEL), lambda b: (b, 0, 0))
    w = pl.BlockSpec((D_MODEL, D_MODEL), lambda b: (0, 0))
    return pl.pallas_call(
        _mem_kv_body,
        out_shape=[jax.ShapeDtypeStruct((bsz, mt, D_MODEL), F32)] * 2,
        grid=(bsz,),
        in_specs=[blk, w, w],
        out_specs=[blk, blk],
        compiler_params=_params(("arbitrary",)),
        name="mem_kv",
    )(mem, wk, wv)


def _mem_attn_body(q_ref, k_ref, v_ref, o_ref, *, g):
    scale = MEM_DH ** -0.5
    for ig in range(g):
        outs = []
        for h in range(MEM_HEADS):
            hs = slice(h * MEM_DH, (h + 1) * MEM_DH)
            s = _mm_nt(q_ref[ig, :, hs], k_ref[ig, :, hs].astype(BF16)) * scale
            m = jnp.max(s, axis=-1, keepdims=True)
            p = jnp.exp(s - m)
            p = p / jnp.sum(p, axis=-1, keepdims=True)
            outs.append(_mm(p.astype(BF16), v_ref[ig, :, hs].astype(BF16)))
        o_ref[ig] = jnp.concatenate(outs, axis=1).astype(BF16)


def _mem_attn(q, mk, mv, *, g, kv_of_step):
    nq, tq, _ = q.shape
    mt = mk.shape[1]
    qs = pl.BlockSpec((g, tq, D_MODEL), lambda i: (i, 0, 0))
    ks = pl.BlockSpec((g, mt, D_MODEL), lambda i: (kv_of_step(i), 0, 0))
    return pl.pallas_call(
        functools.partial(_mem_attn_body, g=g),
        out_shape=jax.ShapeDtypeStruct((nq, tq, D_MODEL), BF16),
        grid=(nq // g,),
        in_specs=[qs, ks, ks],
        out_specs=qs,
        compiler_params=_params(("arbitrary",)),
        name="mem_attn",
    )(q, mk, mv)


def _mem_attn_rows_body(q_ref, k_ref, v_ref, mask_ref, o_ref, *, g):
    scale = MEM_DH ** -0.5
    same_head = mask_ref[...] > 0.0
    for ig in range(g):
        s = _mm_nt(q_ref[ig], k_ref[ig].astype(BF16)) * scale
        s = jnp.where(same_head, s, NEG_INF)
        m = jnp.max(s, axis=-1, keepdims=True)
        p = jnp.exp(s - m)
        p = p / jnp.sum(p, axis=-1, keepdims=True)
        o_ref[ig] = _mm(p.astype(BF16), v_ref[ig].astype(BF16)).astype(BF16)


def _mem_attn_rows(q, mk, mv, *, g):
    nb, qr, _ = q.shape
    kr = mk.shape[1]
    mask = (np.arange(qr)[:, None] % MEM_HEADS == np.arange(kr)[None, :] % MEM_HEADS).astype(np.float32)
    qs = pl.BlockSpec((g, qr, MEM_DH), lambda i: (i, 0, 0))
    ks = pl.BlockSpec((g, kr, MEM_DH), lambda i: (i, 0, 0))
    return pl.pallas_call(
        functools.partial(_mem_attn_rows_body, g=g),
        out_shape=jax.ShapeDtypeStruct((nb, qr, MEM_DH), BF16),
        grid=(nb // g,),
        in_specs=[qs, ks, ks, pl.BlockSpec(mask.shape, lambda i: (0, 0))],
        out_specs=qs,
        compiler_params=_params(("arbitrary",)),
        name="mem_attn_rows",
    )(q, mk, mv, jnp.asarray(mask))


ROUTE_GROUP_LANE0 = N_EXPERTS


def _post_c_body(x1p_ref, aop_ref, x1s_ref, aos_ref, wo_ref, lg_ref, lb_ref, wr_ref, br_ref,
                 x2_o, eid_o, ew_o, *, prompt_steps):
    def run(x1_ref, ao_ref):
        x2 = _layer_norm(ALPHA * x1_ref[...] + _mm(ao_ref[...], wo_ref[...]), lg_ref[...], lb_ref[...])
        x2_o[...] = x2
        logits = _mm(x2.astype(BF16), wr_ref[...]) + br_ref[...]
        lane = lax.broadcasted_iota(jnp.int32, logits.shape, 1).astype(F32)
        big = float(LANES)

        def first_argmax(vals, vmax):
            return jnp.min(jnp.where(vals == vmax, lane, big), axis=-1, keepdims=True)

        is_g = (lane >= ROUTE_GROUP_LANE0) & (lane < ROUTE_GROUP_LANE0 + N_GROUPS)
        gl = jnp.where(is_g, logits, NEG_INF)
        gmax = jnp.max(gl, axis=-1, keepdims=True)
        g_sel = first_argmax(gl, gmax) - ROUTE_GROUP_LANE0
        g_w = 1.0 / jnp.sum(jnp.where(is_g, jnp.exp(gl - gmax), 0.0), axis=-1, keepdims=True)
        lo = g_sel * EXPERTS_PER_GROUP
        el = jnp.where((lane >= lo) & (lane < lo + EXPERTS_PER_GROUP), logits, NEG_INF)
        v1 = jnp.max(el, axis=-1, keepdims=True)
        i1 = first_argmax(el, v1)
        el2 = jnp.where(lane == i1, NEG_INF, el)
        v2 = jnp.max(el2, axis=-1, keepdims=True)
        i2 = first_argmax(el2, v2)
        e = jnp.exp(v2 - v1)
        w1 = g_w / (1.0 + e)
        w2 = g_w * e / (1.0 + e)
        eid_o[...] = jnp.where(lane == 0.0, i1, jnp.where(lane == 1.0, i2, 0.0)).astype(jnp.int32)
        ew_o[...] = jnp.where(lane == 0.0, w1, jnp.where(lane == 1.0, w2, 0.0))

    i = pl.program_id(0)

    @pl.when(i < prompt_steps)
    def _():
        run(x1p_ref, aop_ref)

    @pl.when(i >= prompt_steps)
    def _():
        run(x1s_ref, aos_ref)


def _post_c(x1p, aop, x1s, aos, wo, lg, lb, wr, br):
    n_p, n_s = x1p.shape[0], x1s.shape[0]
    tm = int(np.gcd(np.gcd(n_p, n_s), TOKEN_TILE))
    sp, ss = n_p // tm, n_s // tm
    prow = lambda i: (jnp.minimum(i, sp - 1), 0)
    srow = lambda i: (jnp.maximum(i - sp, 0), 0)
    orow = lambda i: (i, 0)
    const = lambda i: (0, 0)
    full = lambda a: pl.BlockSpec(a.shape, const)
    n_total = n_p + n_s
    return pl.pallas_call(
        functools.partial(_post_c_body, prompt_steps=sp),
        out_shape=[jax.ShapeDtypeStruct((n_total, D_MODEL), F32),
                   jax.ShapeDtypeStruct((n_total, LANES), jnp.int32),
                   jax.ShapeDtypeStruct((n_total, LANES), F32)],
        grid=(sp + ss,),
        in_specs=[pl.BlockSpec((tm, D_MODEL), prow), pl.BlockSpec((tm, D_MODEL), prow),
                  pl.BlockSpec((tm, D_MODEL), srow), pl.BlockSpec((tm, D_MODEL), srow),
                  full(wo), full(lg), full(lb), full(wr), full(br)],
        out_specs=[pl.BlockSpec((tm, D_MODEL), orow), pl.BlockSpec((tm, LANES), orow),
                   pl.BlockSpec((tm, LANES), orow)],
        compiler_params=_params(("arbitrary",)),
        name="post_c",
    )(x1p, aop, x1s, aos, wo, lg, lb, wr, br)


def _moe_body(te_ref, nv_ref, nlive_ref, src0_ref, srcn_ref, dst_ref, x_hbm, wg_ref, wu_ref, wd_ref, eo_hbm,
              xbuf, obuf, src_smem, dst_smem, gsem, ssem, isem, wgb, wub, wdb, *, nt, spare0):
    t = pl.program_id(0)
    tile = MOE_TILE
    slot = t % 2

    def to_smem(idx_vmem, smem):
        cp = pltpu.make_async_copy(idx_vmem.at[0], smem, isem)
        cp.start()
        cp.wait()

    def gather(dst_slot):
        def body(i, carry):
            pltpu.make_async_copy(x_hbm.at[pl.ds(src_smem[0, i], 1)], xbuf.at[dst_slot, pl.ds(i, 1)],
                                  gsem.at[dst_slot]).start()
            return carry
        lax.fori_loop(0, tile, body, 0, unroll=8)

    def scatter_wait(s):
        pltpu.make_async_copy(obuf.at[s], eo_hbm.at[pl.ds(0, tile)], ssem.at[s]).wait()

    @pl.when(t == 0)
    def _():
        obuf[...] = jnp.zeros_like(obuf)
        for s in range(2):
            cp = pltpu.make_async_copy(obuf.at[s], eo_hbm.at[pl.ds(spare0 + s * tile, tile)], ssem.at[s])
            cp.start()
            cp.wait()
        to_smem(src0_ref, src_smem)
        gather(0)

    @pl.when((t + 1 < nt) & (nv_ref[jnp.minimum(t + 1, nt - 1)] > 0))
    def _():
        to_smem(srcn_ref, src_smem)
        gather((t + 1) % 2)

    @pl.when((t == 0) | (te_ref[t] != te_ref[jnp.maximum(t - 1, 0)]))
    def _():
        wgb[...] = wg_ref[...].astype(BF16)
        wub[...] = wu_ref[...].astype(BF16)
        wdb[...] = wd_ref[...].astype(BF16)

    @pl.when(nv_ref[t] > 0)
    def _():
        pltpu.make_async_copy(x_hbm.at[pl.ds(0, tile)], xbuf.at[slot], gsem.at[slot]).wait()

        @pl.when(t >= 2)
        def _():
            scatter_wait(slot)

        xb = xbuf[slot].astype(BF16)
        hg = _mm(xb, wgb[...])
        hu = _mm(xb, wub[...])
        h = (hg * jax.nn.sigmoid(hg) * hu).astype(BF16)
        obuf[slot] = _mm(h, wdb[...])
        to_smem(dst_ref, dst_smem)

        def body(i, carry):
            pltpu.make_async_copy(obuf.at[slot, pl.ds(i, 1)], eo_hbm.at[pl.ds(dst_smem[0, i], 1)],
                                  ssem.at[slot]).start()
            return carry
        lax.fori_loop(0, tile, body, 0, unroll=8)

    @pl.when(t == nt - 1)
    def _():
        live = nlive_ref[0]

        @pl.when(live >= 1)
        def _():
            scatter_wait((live - 1) % 2)

        @pl.when(live >= 2)
        def _():
            scatter_wait(live % 2)


def _moe(tile_expert, n_valid, src, dst, x2, wg, wu, wd, n_tok):
    nt = tile_expert.shape[0]
    tile = MOE_TILE
    n_live = jnp.sum((n_valid > 0).astype(jnp.int32)).reshape(1)
    spare0 = 2 * n_tok
    wspec = lambda shape: pl.BlockSpec((None,) + shape, lambda t, te, nv, nl: (te[t], 0, 0))
    ispec = lambda f: pl.BlockSpec((1, 1, tile), lambda t, te, nv, nl: (f(t), 0, 0))
    grid_spec = pltpu.PrefetchScalarGridSpec(
        num_scalar_prefetch=3,
        grid=(nt,),
        in_specs=[ispec(lambda t: 0), ispec(lambda t: jnp.minimum(t + 1, nt - 1)), ispec(lambda t: t),
                  pl.BlockSpec(memory_space=pl.ANY),
                  wspec((D_MODEL, EXPERT_HIDDEN)), wspec((D_MODEL, EXPERT_HIDDEN)),
                  wspec((EXPERT_HIDDEN, D_MODEL))],
        out_specs=pl.BlockSpec(memory_space=pl.ANY),
        scratch_shapes=[pltpu.VMEM((2, tile, D_MODEL), F32),
                        pltpu.VMEM((2, tile, D_MODEL), F32),
                        pltpu.SMEM((1, tile), jnp.int32),
                        pltpu.SMEM((1, tile), jnp.int32),
                        pltpu.SemaphoreType.DMA((2,)),
                        pltpu.SemaphoreType.DMA((2,)),
                        pltpu.SemaphoreType.DMA,
                        pltpu.VMEM((D_MODEL, EXPERT_HIDDEN), BF16),
                        pltpu.VMEM((D_MODEL, EXPERT_HIDDEN), BF16),
                        pltpu.VMEM((EXPERT_HIDDEN, D_MODEL), BF16)])
    return pl.pallas_call(
        functools.partial(_moe_body, nt=nt, spare0=spare0),
        out_shape=jax.ShapeDtypeStruct((spare0 + 2 * tile, D_MODEL), F32),
        grid_spec=grid_spec,
        compiler_params=_params(("arbitrary",)),
        name="moe",
    )(tile_expert, n_valid, n_live, src, src, dst, x2, wg, wu, wd)


def _final_body(x_ref, ew_ref, e1_ref, e2_ref, lg_ref, lb_ref, y_ref):
    ew = ew_ref[...]
    moe = ew[:, 0:1] * e1_ref[...] + ew[:, 1:2] * e2_ref[...]
    y_ref[...] = _layer_norm(ALPHA * x_ref[...] + moe, lg_ref[...], lb_ref[...])


def _final(x2, ew, eo, lg, lb, row0, t):
    n_tok = x2.shape[0]
    tile = int(np.gcd(np.gcd(row0, t), np.gcd(n_tok, TOKEN_TILE)))
    b0 = row0 // tile
    b1 = n_tok // tile
    const = lambda i: (0, 0)
    blk = lambda off: pl.BlockSpec((tile, D_MODEL), lambda i: (off + i, 0))
    return pl.pallas_call(
        _final_body,
        out_shape=jax.ShapeDtypeStruct((t, D_MODEL), F32),
        grid=(t // tile,),
        in_specs=[blk(b0), pl.BlockSpec((tile, LANES), lambda i: (b0 + i, 0)), blk(b0), blk(b1 + b0),
                  pl.BlockSpec(lg.shape, const), pl.BlockSpec(lb.shape, const)],
        out_specs=blk(0),
        compiler_params=_params(("arbitrary",)),
        name="final",
    )(x2, ew, eo, eo, lg, lb)


def _routing_tables(eid, n_tok):
    tile = MOE_TILE
    n_assign = 2 * n_tok
    assert n_assign % tile == 0
    nt = n_assign // tile + N_EXPERTS
    flat = eid.reshape(-1).astype(jnp.int32)
    experts = jnp.arange(N_EXPERTS, dtype=jnp.int32)
    counts = jnp.sum((flat[:, None] == experts[None, :]).astype(jnp.int32), axis=0)
    pad = (-counts) % tile
    unused = 2 * N_EXPERTS
    pad_keys = jnp.where(jnp.arange(tile, dtype=jnp.int32)[None, :] < pad[:, None],
                         2 * experts[:, None] + 1, unused).reshape(-1)
    keys = jnp.concatenate([2 * flat, pad_keys])
    vals = jnp.concatenate([jnp.arange(n_assign, dtype=jnp.int32),
                            jnp.full((N_EXPERTS * tile,), -1, jnp.int32)])
    keys, vals = lax.sort((keys, vals), num_keys=1)
    keys = keys.reshape(nt, tile)
    vals = vals.reshape(nt, tile)
    tile_expert = jnp.minimum(keys[:, 0] // 2, N_EXPERTS - 1)
    real = vals >= 0
    n_valid = jnp.sum(real.astype(jnp.int32), axis=1)
    a = jnp.maximum(vals, 0)
    src = a >> 1
    spare = 2 * n_tok + (jnp.arange(nt, dtype=jnp.int32)[:, None] % 2) * tile + jnp.arange(tile, dtype=jnp.int32)[None, :]
    dst = jnp.where(real, (a & 1) * n_tok + (a >> 1), spare)
    return tile_expert, n_valid, src.reshape(nt, 1, tile), dst.reshape(nt, 1, tile)


def kernel(x_prompt, x_sample, mem_prompt, cache_dil_k, cache_dil_v, state_gla, cache_mem_k, cache_mem_v, w_in, w_gate_lr, b_gate, g_gla_norm, w_out, ln_mix_g, ln_mix_b, w_mem_q, w_mem_k, w_mem_v, w_mem_o, ln_mem_g, ln_mem_b, w_route_group, b_route_group, w_route_expert, b_route_expert, w_exp_gate, w_exp_up, w_exp_down, ln_ffn_g, ln_ffn_b):
    assert w_in.shape[0] == DEPTH
    bp, seq, _ = x_prompt.shape
    db, tdec, _ = x_sample.shape
    buf = cache_dil_k.shape[2]
    mt = mem_prompt.shape[1]
    n_p, n_s = bp * seq, db * tdec
    n_tok = n_p + n_s
    l = 0

    a0 = 2 * GLA_KEY_W + 2 * GLA_VAL_W
    w_main = jnp.concatenate([w_in[l][:, :a0], w_in[l][:, a0 + GATE_RANK:]], axis=1).astype(BF16)
    w_a = jnp.pad(w_in[l][:, a0:a0 + GATE_RANK], ((0, 0), (0, LANES - GATE_RANK))).astype(BF16)
    w_gl = jnp.pad(w_gate_lr[l], ((0, LANES - GATE_RANK), (0, 0))).astype(BF16)
    b_g = b_gate[l][None, :]
    row1 = lambda a: a[l][None, :]
    w_o = w_out[l].astype(BF16)
    w_q = w_mem_q[l].astype(BF16)
    w_k = w_mem_k[l].astype(BF16)
    w_v = w_mem_v[l].astype(BF16)
    w_mo = w_mem_o[l].astype(BF16)
    w_r = jnp.pad(jnp.concatenate([w_route_expert[l], w_route_group[l]], axis=1),
                  ((0, 0), (0, LANES - N_EXPERTS - N_GROUPS))).astype(BF16)
    b_r = jnp.pad(jnp.concatenate([b_route_expert[l], b_route_group[l]]), (0, LANES - N_EXPERTS - N_GROUPS))[None, :]

    def mixer_inputs(x2d):
        return _proj(x2d, w_main, w_a, w_gl, b_g)

    def post_a(x2d, o_g, r_g, o_d):
        return _post_a(x2d, o_g, r_g, o_d, row1(g_gla_norm), w_o, row1(ln_mix_g), row1(ln_mix_b), w_q)

    xp = x_prompt.reshape(n_p, D_MODEL)
    qg, kg, vg, rg, la, qd, kd, vd = mixer_inputs(xp)
    sh = lambda t: t.reshape(bp, seq, t.shape[-1])
    s0 = jnp.zeros((bp, H_GLA, GLA_DK, GLA_DV), F32)
    o_g, s_fin_p = _gla(sh(qg), sh(kg), sh(vg), sh(la), s0, bb=1, nchunk=4)
    o_d = _dil_prompt(sh(qd), sh(kd), sh(vd))
    mem_k, mem_v = _mem_kv(mem_prompt, w_k, w_v)
    x1_p, qm = post_a(xp, o_g.reshape(n_p, -1), rg, o_d.reshape(n_p, -1))
    tq = min(TOKEN_TILE, seq)
    steps_per_b = seq // tq
    ao_p = _mem_attn(qm.reshape(n_p // tq, tq, D_MODEL), mem_k, mem_v, g=1,
                     kv_of_step=lambda i: i // steps_per_b).reshape(n_p, D_MODEL)
    keep = min(WINDOW_MAX, seq)
    dk_p = sh(kd)[:, seq - keep:].reshape(1, bp, keep, H_DIL, DIL_DH)
    dv_p = sh(vd)[:, seq - keep:].reshape(1, bp, keep, H_DIL, DIL_DH)

    xs = x_sample.reshape(n_s, D_MODEL)
    qg, kg, vg, rg, la, qd, kd, vd = mixer_inputs(xs)
    shs = lambda t: t.reshape(db, tdec, t.shape[-1])
    gb = 8 if db % 8 == 0 else 1
    o_g, s_new = _gla(shs(qg), shs(kg), shs(vg), shs(la), state_gla[l], bb=gb, nchunk=1)
    head_rows = lambda t: t.reshape(db, tdec * H_DIL, DIL_DH)
    groups = lambda c: c[l].reshape(db, buf // DIL_GROUP, DIL_GROUP * H_DIL, DIL_DH)
    nk, nv, o_d = _dil_sample(head_rows(qd), head_rows(kd), head_rows(vd),
                              groups(cache_dil_k), groups(cache_dil_v))
    x1_s, qm = post_a(xs, o_g.reshape(n_s, -1), rg, o_d.reshape(n_s, DIL_W))
    gm = 4 if db % 4 == 0 else 1
    ao_s = _mem_attn_rows(qm.reshape(db, tdec * MEM_HEADS, MEM_DH),
                          cache_mem_k[l].reshape(db, mt * MEM_HEADS, MEM_DH),
                          cache_mem_v[l].reshape(db, mt * MEM_HEADS, MEM_DH), g=gm).reshape(n_s, D_MODEL)

    x2, eid, ew = _post_c(x1_p, ao_p, x1_s, ao_s, w_mo, row1(ln_mem_g), row1(ln_mem_b), w_r, b_r)
    tile_expert, n_valid, src, dst = _routing_tables(eid[:, :2], n_tok)
    eo = _moe(tile_expert, n_valid, src, dst, x2, w_exp_gate[l], w_exp_up[l], w_exp_down[l], n_tok)
    y_p = _final(x2, ew, eo, row1(ln_ffn_g), row1(ln_ffn_b), 0, n_p)
    y_s = _final(x2, ew, eo, row1(ln_ffn_g), row1(ln_ffn_b), n_p, n_s)

    return (y_p.reshape(bp, seq, D_MODEL), y_s.reshape(db, tdec, D_MODEL),
            dk_p, dv_p, s_fin_p[None], mem_k.reshape(1, bp, mt, MEM_HEADS, MEM_DH),
            mem_v.reshape(1, bp, mt, MEM_HEADS, MEM_DH),
            nk.reshape(1, db, buf, H_DIL, DIL_DH), nv.reshape(1, db, buf, H_DIL, DIL_DH), s_new[None])
```

```python
import functools

import numpy as np
import jax
import jax.numpy as jnp
from jax import lax
from jax.experimental import pallas as pl
from jax.experimental.pallas import tpu as pltpu

F32 = jnp.float32
BF16 = jnp.bfloat16

D_MODEL = 1024
H_GLA = 4
GLA_DK = 64
GLA_DV = 128
GLA_KEY_W = H_GLA * GLA_DK
GLA_VAL_W = H_GLA * GLA_DV
GATE_RANK = 16
GATE_TAU = 16.0
GLA_CHUNK = 64
H_DIL = 4
DIL_DH = 128
DIL_W = H_DIL * DIL_DH
DIL_PAIRS = ((128, 1), (512, 4), (2048, 16))
DIL_SPAN = 128
WINDOW_MAX = 2048
MEM_HEADS = 4
MEM_DH = 256
N_GROUPS = 4
EXPERTS_PER_GROUP = 8
N_EXPERTS = N_GROUPS * EXPERTS_PER_GROUP
EXPERT_HIDDEN = 512
DEPTH = 1
ALPHA = (2.0 * DEPTH) ** 0.25
LN_EPS = 1e-5
NEG_INF = -1e30

LANES = 128
VMEM_LIMIT = 56 * 1024 * 1024

TOKEN_TILE = 512
MOE_TILE = 256


def _mm(a, b):
    return jnp.dot(a, b, preferred_element_type=F32)


def _mm_nt(a, b):
    return lax.dot_general(a, b, (((1,), (1,)), ((), ())), preferred_element_type=F32)


def _mm_tn(a, b):
    return lax.dot_general(a, b, (((0,), (0,)), ((), ())), preferred_element_type=F32)


def _layer_norm(v, g, b):
    mu = jnp.mean(v, axis=-1, keepdims=True)
    d = v - mu
    var = jnp.mean(d * d, axis=-1, keepdims=True)
    return d * lax.rsqrt(var + LN_EPS) * g + b


def _params(sem):
    return pltpu.CompilerParams(dimension_semantics=sem, vmem_limit_bytes=VMEM_LIMIT)


ROW_TILES = D_MODEL // LANES


def _to_token_tiles(ref, val, base=0):
    n = val.shape[0]
    for s in range(ROW_TILES):
        ref[pl.ds(base + s, n, stride=ROW_TILES), :] = val[:, s * LANES:(s + 1) * LANES]


def _from_token_tiles(ref, n, base=0):
    return jnp.concatenate([ref[pl.ds(base + s, n, stride=ROW_TILES), :] for s in range(ROW_TILES)], axis=1)


def _proj_body(x_ref, w_ref, wa_ref, wgl_ref, bg_ref,
               qg_o, kg_o, vg_o, r_o, la_o, qd_o, kd_o, vd_o):
    xb = x_ref[...].astype(BF16)

    def mm(lo, hi):
        return _mm(xb, w_ref[:, lo:hi])

    qg_o[...] = mm(0, 256) * (GLA_DK ** -0.5)
    kg_o[...] = mm(256, 512)
    vg_o[...] = mm(512, 1024)
    r_o[...] = mm(1024, 1536)
    qd_o[...] = mm(1536, 2048)
    kd_o[...] = mm(2048, 2560)
    vd_o[...] = mm(2560, 3072)
    a_lr = _mm(xb, wa_ref[...])
    z = _mm(a_lr.astype(BF16), wgl_ref[...]) + bg_ref[...]
    la_o[...] = (jnp.minimum(z, 0.0) - jnp.log1p(jnp.exp(-jnp.abs(z)))) * (1.0 / GATE_TAU)


def _proj(x, w_main, w_a, w_gl, b_g):
    t = x.shape[0]
    tm = min(TOKEN_TILE, t)
    widths = (256, 256, 512, 512, 256, 512, 512, 512)
    row = lambda i: (i, 0)
    const = lambda i: (0, 0)
    return pl.pallas_call(
        _proj_body,
        out_shape=[jax.ShapeDtypeStruct((t, w), F32) for w in widths],
        grid=(t // tm,),
        in_specs=[pl.BlockSpec((tm, D_MODEL), row),
                  pl.BlockSpec(w_main.shape, const),
                  pl.BlockSpec(w_a.shape, const),
                  pl.BlockSpec(w_gl.shape, const),
                  pl.BlockSpec(b_g.shape, const)],
        out_specs=[pl.BlockSpec((tm, w), row) for w in widths],
        compiler_params=_params(("arbitrary",)),
        name="proj",
    )(x, w_main, w_a, w_gl, b_g)


def _split3(a):
    a1 = a.astype(BF16)
    r1 = a - a1.astype(F32)
    a2 = r1.astype(BF16)
    r2 = r1 - a2.astype(F32)
    return a1, a2, r2.astype(BF16)


def _gla_body(q_ref, k_ref, v_ref, la_ref, s0_ref, o_ref, sfin_ref, s_scr, *, bb, rows, nchunk):
    c = GLA_CHUNK
    j = pl.program_id(1)

    @pl.when(j == 0)
    def _():
        s_scr[...] = s0_ref[...]

    ri = lax.broadcasted_iota(jnp.int32, (c, c), 0)
    ci = lax.broadcasted_iota(jnp.int32, (c, c), 1)
    causal = ci <= ri
    tri = jnp.where(causal, 1.0, 0.0).astype(BF16)
    ones = jnp.ones((c, GLA_DV), BF16)

    def pad(t):
        if rows == c * nchunk:
            return t
        return jnp.concatenate([t, jnp.zeros((c * nchunk - rows, t.shape[1]), t.dtype)], axis=0)

    for ib in range(bb):
        q_all = pad(q_ref[ib])
        k_all = pad(k_ref[ib])
        v_all = pad(v_ref[ib])
        la_all = pad(la_ref[ib])
        for ch in range(nchunk):
            sl = slice(ch * c, (ch + 1) * c)
            a = la_all[sl]
            p1, p2, p3 = _split3(a)
            cum = _mm(tri, jnp.concatenate([p1, p2, p3], axis=1))
            b = cum[:, :GLA_KEY_W] + cum[:, GLA_KEY_W:2 * GLA_KEY_W] + cum[:, 2 * GLA_KEY_W:]
            outs = []
            for h in range(H_GLA):
                hk = slice(h * GLA_DK, (h + 1) * GLA_DK)
                hv = slice(h * GLA_DV, (h + 1) * GLA_DV)
                bh = b[:, hk]
                qh = q_all[sl, hk]
                kh = k_all[sl, hk]
                vh = v_all[sl, hv].astype(BF16)
                b_last = bh[c - 1:c, :]
                q_t = (qh * jnp.exp(bh)).astype(BF16)
                k_t = (kh * jnp.exp(-bh)).astype(BF16)
                k_end = (kh * jnp.exp(b_last - bh)).astype(BF16)
                a_in = jnp.where(causal, _mm_nt(q_t, k_t), 0.0).astype(BF16)
                s = s_scr[ib, h]
                outs.append(_mm(a_in, vh) + _mm(q_t, s.astype(BF16)))
                pieces = jnp.concatenate([p1[:, hk], p2[:, hk], p3[:, hk]], axis=1)
                bl = _mm_tn(pieces, ones)
                bl = bl[:GLA_DK] + bl[GLA_DK:2 * GLA_DK] + bl[2 * GLA_DK:]
                s_scr[ib, h] = jnp.exp(bl) * s + _mm_tn(k_end, vh)
            o_chunk = jnp.concatenate(outs, axis=1)
            if rows == c * nchunk:
                o_ref[ib, sl, :] = o_chunk
            else:
                o_ref[ib] = o_chunk[:rows]

    @pl.when(j == pl.num_programs(1) - 1)
    def _():
        sfin_ref[...] = s_scr[...]


def _gla(q, k, v, la, s0, *, bb, nchunk):
    nb, s, _ = q.shape
    rows = min(s, GLA_CHUNK * nchunk)
    assert s % rows == 0 and nb % bb == 0
    blk = lambda i, j: (i, j, 0)
    st = lambda i, j: (i, 0, 0, 0)
    body = functools.partial(_gla_body, bb=bb, rows=rows, nchunk=nchunk)
    return pl.pallas_call(
        body,
        out_shape=[jax.ShapeDtypeStruct((nb, s, GLA_VAL_W), F32),
                   jax.ShapeDtypeStruct((nb, H_GLA, GLA_DK, GLA_DV), F32)],
        grid=(nb // bb, s // rows),
        in_specs=[pl.BlockSpec((bb, rows, GLA_KEY_W), blk),
                  pl.BlockSpec((bb, rows, GLA_KEY_W), blk),
                  pl.BlockSpec((bb, rows, GLA_VAL_W), blk),
                  pl.BlockSpec((bb, rows, GLA_KEY_W), blk),
                  pl.BlockSpec((bb, H_GLA, GLA_DK, GLA_DV), st)],
        out_specs=[pl.BlockSpec((bb, rows, GLA_VAL_W), blk),
                   pl.BlockSpec((bb, H_GLA, GLA_DK, GLA_DV), st)],
        scratch_shapes=[pltpu.VMEM((bb, H_GLA, GLA_DK, GLA_DV), F32)],
        compiler_params=_params(("arbitrary", "arbitrary")),
        name="gla",
    )(q, k, v, la, s0)


DIL_QBLK = 128
DIL_TILE = 2048
DIL_UNROLL = 4


def _dil_body(q_ref, k_ref, v_ref, o_ref, acc_ref, m_ref, l_ref, *, tile):
    blk = DIL_QBLK
    t0 = pl.program_id(2) * tile
    scale = DIL_DH ** -0.5
    ri = lax.broadcasted_iota(jnp.int32, (blk, 2 * blk), 0)
    ci = lax.broadcasted_iota(jnp.int32, (blk, 2 * blk), 1)
    band = (ci >= ri) & (ci <= ri + DIL_SPAN)

    def rows(start, dil):
        return pl.ds(start, blk) if dil == 1 else pl.ds(start, blk, stride=dil)

    def block(dil, q0, first, last):
        qr = rows(q0, dil)
        cur0 = t0 + q0
        prev0 = cur0 - dil * blk
        has_prev = prev0 >= 0
        prev0 = jnp.maximum(prev0, 0)
        kw = jnp.concatenate([k_ref[rows(prev0, dil), :], k_ref[rows(cur0, dil), :]], axis=0)
        vw = jnp.concatenate([v_ref[rows(prev0, dil), :], v_ref[rows(cur0, dil), :]], axis=0)
        s = _mm_nt(q_ref[qr, :].astype(BF16), kw.astype(BF16)) * scale
        s = jnp.where(band & ((ci >= blk) | has_prev), s, NEG_INF)
        m_new = jnp.max(s, axis=-1, keepdims=True)
        p = jnp.exp(s - m_new)
        l_new = jnp.sum(p, axis=-1, keepdims=True)
        acc = _mm(p.astype(BF16), vw.astype(BF16))
        if not first:
            m_blk, m_old = m_new, m_ref[qr, :][:, 0:1]
            m_new = jnp.maximum(m_old, m_blk)
            a_old = jnp.exp(m_old - m_new)
            a_blk = jnp.exp(m_blk - m_new)
            l_new = a_old * l_ref[qr, :][:, 0:1] + a_blk * l_new
            acc = a_old * acc_ref[qr, :] + a_blk * acc
        if last:
            o_ref[qr, :] = acc / l_new
        else:
            acc_ref[qr, :] = acc
            m_ref[qr, :] = jnp.broadcast_to(m_new, (blk, LANES))
            l_ref[qr, :] = jnp.broadcast_to(l_new, (blk, LANES))

    dils = sorted((d for _, d in DIL_PAIRS), reverse=True)
    for bi, dil in enumerate(dils):
        per_res = tile // (dil * blk)

        def body(i, carry, dil=dil, per_res=per_res, bi=bi):
            q0 = (i % dil) + (i // dil) * (dil * blk) if dil > 1 else pl.multiple_of(i * blk, blk)
            block(dil, q0, bi == 0, bi == len(dils) - 1)
            return carry

        lax.fori_loop(0, dil * per_res, body, 0, unroll=DIL_UNROLL)


def _dil_prompt(q, k, v):
    bsz, s, _ = q.shape
    tile = min(DIL_TILE, s)
    assert s % tile == 0 and all(tile % (d * DIL_QBLK) == 0 for _, d in DIL_PAIRS)
    assert all(w // d == DIL_SPAN for w, d in DIL_PAIRS)
    qs = pl.BlockSpec((None, tile, DIL_DH), lambda b, h, n: (b, n, h))
    ks = pl.BlockSpec((None, s, DIL_DH), lambda b, h, n: (b, 0, h))
    return pl.pallas_call(
        functools.partial(_dil_body, tile=tile),
        out_shape=jax.ShapeDtypeStruct((bsz, s, DIL_W), F32),
        grid=(bsz, H_DIL, s // tile),
        in_specs=[qs, ks, ks],
        out_specs=qs,
        scratch_shapes=[pltpu.VMEM((tile, DIL_DH), F32), pltpu.VMEM((tile, LANES), F32),
                        pltpu.VMEM((tile, LANES), F32)],
        compiler_params=_params(("arbitrary", "arbitrary", "arbitrary")),
        name="dil_prompt",
    )(q, k, v)


DIL_GROUP = 16
DIL_TAIL = 512


def _dil_sample_counts(buf, t_new):
    ga = (buf - DIL_TAIL) // DIL_GROUP
    pos_all = np.arange(buf)
    sel = (pos_all >= ga * DIL_GROUP) | (pos_all % DIL_GROUP < t_new)
    t = np.arange(t_new)

    def count(pos, real):
        cnt = np.zeros((t_new, pos.size), np.float32)
        for window, dil in DIL_PAIRS:
            delta = buf + t[:, None] - pos[None, :]
            cnt += (delta >= 0) & (delta % dil == 0) & (delta // dil <= window // dil) & real[None, :]
        return cnt

    assert not count(pos_all[~sel], np.ones((~sel).sum(), bool)).any()
    pad_new = LANES // H_DIL
    pos = np.concatenate([pos_all[sel], buf + np.arange(pad_new)])
    real = np.concatenate([np.ones(sel.sum(), bool), np.arange(pad_new) < t_new])
    cnt = count(pos, real)
    same_head = np.eye(H_DIL, dtype=np.float32)
    full = cnt[:, None, :, None] * same_head[None, :, None, :]
    return full.reshape(t_new * H_DIL, pos.size * H_DIL)


def _dil_sample_body(q_ref, kn_ref, vn_ref, ck_ref, cv_ref, cnt_ref, nk_ref, nv_ref, o_ref, *, ga):
    g = ck_ref.shape[0]
    half = ck_ref.shape[1] // 2
    k_new = kn_ref[...]
    v_new = vn_ref[...]
    for c_ref, n_ref, new in ((ck_ref, nk_ref, k_new), (cv_ref, nv_ref, v_new)):
        n_ref[:, 0:half, :] = c_ref[:, half:, :]
        n_ref[0:g - 1, half:, :] = c_ref[1:g, 0:half, :]
        n_ref[g - 1, half:, :] = new

    zpad = jnp.zeros((LANES - half, DIL_DH), F32)

    def keys(c_ref, new):
        main = c_ref[0:ga, 0:half, :].reshape(ga * half, DIL_DH)
        tail = c_ref[ga:g, :, :].reshape((g - ga) * 2 * half, DIL_DH)
        return jnp.concatenate([main, tail, new, zpad], axis=0).astype(BF16)

    s = _mm_nt(q_ref[...].astype(BF16), keys(ck_ref, k_new)) * (DIL_DH ** -0.5)
    cnt = cnt_ref[...]
    s = jnp.where(cnt > 0.0, s, NEG_INF)
    m = jnp.max(s, axis=-1, keepdims=True)
    p = jnp.exp(s - m) * cnt
    den = jnp.sum(p, axis=-1, keepdims=True)
    o_ref[...] = _mm(p.astype(BF16), keys(cv_ref, v_new)) / den


def _dil_sample(q, k_new, v_new, cache_k, cache_v):
    db, rows_new, _ = q.shape
    g, grows = cache_k.shape[1], cache_k.shape[2]
    assert grows == DIL_GROUP * H_DIL and rows_new * 2 == grows
    buf = g * DIL_GROUP
    ga = (buf - DIL_TAIL) // DIL_GROUP
    cnt = jnp.asarray(_dil_sample_counts(buf, rows_new // H_DIL))
    new = pl.BlockSpec((None, rows_new, DIL_DH), lambda b: (b, 0, 0))
    big = pl.BlockSpec((None, g, grows, DIL_DH), lambda b: (b, 0, 0, 0))
    return pl.pallas_call(
        functools.partial(_dil_sample_body, ga=ga),
        out_shape=[jax.ShapeDtypeStruct(cache_k.shape, F32),
                   jax.ShapeDtypeStruct(cache_v.shape, F32),
                   jax.ShapeDtypeStruct((db, rows_new, DIL_DH), F32)],
        grid=(db,),
        in_specs=[new, new, new, big, big, pl.BlockSpec(cnt.shape, lambda b: (0, 0))],
        out_specs=[big, big, new],
        compiler_params=_params(("arbitrary",)),
        name="dil_sample",
    )(q, k_new, v_new, cache_k, cache_v, cnt)


def _post_a_body(x_ref, og_ref, r_ref, od_ref, gg_ref, wo_ref, lg_ref, lb_ref, wq_ref, x1_o, qm_o):
    og = og_ref[...]
    parts = []
    for h in range(H_GLA):
        oh = og[:, h * GLA_DV:(h + 1) * GLA_DV]
        parts.append(oh * lax.rsqrt(jnp.mean(oh * oh, axis=-1, keepdims=True) + LN_EPS))
    r = r_ref[...]
    on = jnp.concatenate(parts, axis=1) * gg_ref[...] * (r * jax.nn.sigmoid(r))
    cat = jnp.concatenate([on.astype(BF16), od_ref[...].astype(BF16)], axis=1)
    mix = _mm(cat, wo_ref[...])
    x1 = _layer_norm(ALPHA * x_ref[...] + mix, lg_ref[...], lb_ref[...])
    x1_o[...] = x1
    qm_o[...] = _mm(x1.astype(BF16), wq_ref[...]).astype(BF16)


def _post_a(x, og, r, od, gg, wo, lg, lb, wq):
    t = x.shape[0]
    tm = min(TOKEN_TILE, t)
    row = lambda i: (i, 0)
    const = lambda i: (0, 0)
    full = lambda a: pl.BlockSpec(a.shape, const)
    return pl.pallas_call(
        _post_a_body,
        out_shape=[jax.ShapeDtypeStruct((t, D_MODEL), F32), jax.ShapeDtypeStruct((t, D_MODEL), BF16)],
        grid=(t // tm,),
        in_specs=[pl.BlockSpec((tm, D_MODEL), row), pl.BlockSpec((tm, GLA_VAL_W), row),
                  pl.BlockSpec((tm, GLA_VAL_W), row), pl.BlockSpec((tm, DIL_W), row),
                  full(gg), full(wo), full(lg), full(lb), full(wq)],
        out_specs=[pl.BlockSpec((tm, D_MODEL), row), pl.BlockSpec((tm, D_MODEL), row)],
        compiler_params=_params(("arbitrary",)),
        name="post_a",
    )(x, og, r, od, gg, wo, lg, lb, wq)


def _mem_kv_body(m_ref, wk_ref, wv_ref, k_o, v_o):
    mb = m_ref[...].astype(BF16)
    k_o[...] = _mm(mb, wk_ref[...])
    v_o[...] = _mm(mb, wv_ref[...])


def _mem_kv(mem, wk, wv):
    bsz, mt, _ = mem.shape
    blk = pl.BlockSpec((None, mt, D_MODEL), lambda b: (b, 0, 0))
    w = pl.BlockSpec((D_MODEL, D_MODEL), lambda b: (0, 0))
    return pl.pallas_call(
        _mem_kv_body,
        out_shape=[jax.ShapeDtypeStruct((bsz, mt, D_MODEL), F32)] * 2,
        grid=(bsz,),
        in_specs=[blk, w, w],
        out_specs=[blk, blk],
        compiler_params=_params(("arbitrary",)),
        name="mem_kv",
    )(mem, wk, wv)


def _mem_attn_body(q_ref, k_ref, v_ref, o_ref, *, g):
    scale = MEM_DH ** -0.5
    for ig in range(g):
        outs = []
        for h in range(MEM_HEADS):
            hs = slice(h * MEM_DH, (h + 1) * MEM_DH)
            s = _mm_nt(q_ref[ig, :, hs], k_ref[ig, :, hs].astype(BF16)) * scale
            m = jnp.max(s, axis=-1, keepdims=True)
            p = jnp.exp(s - m)
            p = p / jnp.sum(p, axis=-1, keepdims=True)
            outs.append(_mm(p.astype(BF16), v_ref[ig, :, hs].astype(BF16)))
        o_ref[ig] = jnp.concatenate(outs, axis=1).astype(BF16)


def _mem_attn(q, mk, mv, *, g, kv_of_step):
    nq, tq, _ = q.shape
    mt = mk.shape[1]
    qs = pl.BlockSpec((g, tq, D_MODEL), lambda i: (i, 0, 0))
    ks = pl.BlockSpec((g, mt, D_MODEL), lambda i: (kv_of_step(i), 0, 0))
    return pl.pallas_call(
        functools.partial(_mem_attn_body, g=g),
        out_shape=jax.ShapeDtypeStruct((nq, tq, D_MODEL), BF16),
        grid=(nq // g,),
        in_specs=[qs, ks, ks],
        out_specs=qs,
        compiler_params=_params(("arbitrary",)),
        name="mem_attn",
    )(q, mk, mv)


MEM_HALVES = MEM_DH // LANES


def _mem_attn_rows_body(q_ref, k_ref, v_ref, mask_ref, o_ref, *, g):
    scale = MEM_DH ** -0.5
    hh = MEM_HEADS
    nq = q_ref.shape[1] // MEM_HALVES
    nk = k_ref.shape[1]
    valid = mask_ref[...] > 0.0
    for ig in range(g):
        part = _mm_nt(q_ref[ig], k_ref[ig].astype(BF16))
        s = part[0:nq] + pltpu.roll(part[nq:2 * nq], nk - hh, 1)
        s = jnp.where(valid, s * scale, NEG_INF)
        m = jnp.max(s, axis=-1, keepdims=True)
        p = jnp.exp(s - m)
        p = (p / jnp.sum(p, axis=-1, keepdims=True)).astype(F32)
        both = jnp.concatenate([p, pltpu.roll(p, hh, 1)], axis=0).astype(BF16)
        o_ref[ig] = _mm(both, v_ref[ig].astype(BF16)).astype(BF16)


def _mem_attn_rows(q, mk, mv, *, g):
    nb, qr, _ = q.shape
    kr = mk.shape[1]
    nq = qr // MEM_HALVES
    col = np.arange(kr)
    mask = ((col[None, :] % MEM_HEADS == np.arange(nq)[:, None] % MEM_HEADS)
            & (col[None, :] % (MEM_HEADS * MEM_HALVES) < MEM_HEADS)).astype(np.float32)
    qs = pl.BlockSpec((g, qr, LANES), lambda i: (i, 0, 0))
    ks = pl.BlockSpec((g, kr, LANES), lambda i: (i, 0, 0))
    return pl.pallas_call(
        functools.partial(_mem_attn_rows_body, g=g),
        out_shape=jax.ShapeDtypeStruct((nb, qr, LANES), BF16),
        grid=(nb // g,),
        in_specs=[qs, ks, ks, pl.BlockSpec(mask.shape, lambda i: (0, 0))],
        out_specs=qs,
        compiler_params=_params(("arbitrary",)),
        name="mem_attn_rows",
    )(q, mk, mv, jnp.asarray(mask))


ROUTE_GROUP_LANE0 = N_EXPERTS


def _post_c_body(x1p_ref, aop_ref, x1s_ref, aos_ref, wo_ref, lg_ref, lb_ref, wr_ref, br_ref,
                 x2_o, x2t_o, eid_o, ew_o, *, prompt_steps):
    def run(x1_ref, ao_ref):
        x2 = _layer_norm(ALPHA * x1_ref[...] + _mm(ao_ref[...], wo_ref[...]), lg_ref[...], lb_ref[...])
        x2_o[...] = x2
        _to_token_tiles(x2t_o, x2)
        logits = _mm(x2.astype(BF16), wr_ref[...]) + br_ref[...]
        lane = lax.broadcasted_iota(jnp.int32, logits.shape, 1).astype(F32)
        big = float(LANES)

        def first_argmax(vals, vmax):
            return jnp.min(jnp.where(vals == vmax, lane, big), axis=-1, keepdims=True)

        is_g = (lane >= ROUTE_GROUP_LANE0) & (lane < ROUTE_GROUP_LANE0 + N_GROUPS)
        gl = jnp.where(is_g, logits, NEG_INF)
        gmax = jnp.max(gl, axis=-1, keepdims=True)
        g_sel = first_argmax(gl, gmax) - ROUTE_GROUP_LANE0
        g_w = 1.0 / jnp.sum(jnp.where(is_g, jnp.exp(gl - gmax), 0.0), axis=-1, keepdims=True)
        lo = g_sel * EXPERTS_PER_GROUP
        el = jnp.where((lane >= lo) & (lane < lo + EXPERTS_PER_GROUP), logits, NEG_INF)
        v1 = jnp.max(el, axis=-1, keepdims=True)
        i1 = first_argmax(el, v1)
        el2 = jnp.where(lane == i1, NEG_INF, el)
        v2 = jnp.max(el2, axis=-1, keepdims=True)
        i2 = first_argmax(el2, v2)
        e = jnp.exp(v2 - v1)
        w1 = g_w / (1.0 + e)
        w2 = g_w * e / (1.0 + e)
        eid_o[...] = jnp.where(lane == 0.0, i1, jnp.where(lane == 1.0, i2, 0.0)).astype(jnp.int32)
        ew_o[...] = jnp.where(lane == 0.0, w1, jnp.where(lane == 1.0, w2, 0.0))

    i = pl.program_id(0)

    @pl.when(i < prompt_steps)
    def _():
        run(x1p_ref, aop_ref)

    @pl.when(i >= prompt_steps)
    def _():
        run(x1s_ref, aos_ref)


def _post_c(x1p, aop, x1s, aos, wo, lg, lb, wr, br):
    n_p, n_s = x1p.shape[0], x1s.shape[0]
    tm = int(np.gcd(np.gcd(n_p, n_s), TOKEN_TILE))
    sp, ss = n_p // tm, n_s // tm
    prow = lambda i: (jnp.minimum(i, sp - 1), 0)
    srow = lambda i: (jnp.maximum(i - sp, 0), 0)
    orow = lambda i: (i, 0)
    const = lambda i: (0, 0)
    full = lambda a: pl.BlockSpec(a.shape, const)
    n_total = n_p + n_s
    return pl.pallas_call(
        functools.partial(_post_c_body, prompt_steps=sp),
        out_shape=[jax.ShapeDtypeStruct((n_total, D_MODEL), F32),
                   jax.ShapeDtypeStruct((n_total * ROW_TILES, LANES), F32),
                   jax.ShapeDtypeStruct((n_total, LANES), jnp.int32),
                   jax.ShapeDtypeStruct((n_total, LANES), F32)],
        grid=(sp + ss,),
        in_specs=[pl.BlockSpec((tm, D_MODEL), prow), pl.BlockSpec((tm, D_MODEL), prow),
                  pl.BlockSpec((tm, D_MODEL), srow), pl.BlockSpec((tm, D_MODEL), srow),
                  full(wo), full(lg), full(lb), full(wr), full(br)],
        out_specs=[pl.BlockSpec((tm, D_MODEL), orow), pl.BlockSpec((tm * ROW_TILES, LANES), orow),
                   pl.BlockSpec((tm, LANES), orow), pl.BlockSpec((tm, LANES), orow)],
        compiler_params=_params(("arbitrary",)),
        name="post_c",
    )(x1p, aop, x1s, aos, wo, lg, lb, wr, br)


def _moe_body(te_ref, nv_ref, nlive_ref, src0_ref, src1_ref, srcn_ref, dst_ref, x_hbm, wg_ref, wu_ref, wd_ref, eo_hbm,
              xbuf, obuf, src_smem, dst_smem, gsem, ssem, isem, dsem, wgb, wub, wdb, *, nt, spare0):
    t = pl.program_id(0)
    tile = MOE_TILE
    slot = t % 2
    rt = ROW_TILES

    def src_copy(idx_vmem, s):
        return pltpu.make_async_copy(idx_vmem.at[0], src_smem.at[s], isem.at[s])

    def dst_copy():
        return pltpu.make_async_copy(dst_ref.at[0], dst_smem, dsem)

    def gather(s):
        def body(i, carry):
            row = pl.multiple_of(src_smem[s, 0, i], rt)
            pltpu.make_async_copy(x_hbm.at[pl.ds(row, rt)], xbuf.at[s, pl.ds(pl.multiple_of(i * rt, rt), rt)],
                                  gsem.at[s]).start()
            return carry
        lax.fori_loop(0, tile, body, 0, unroll=16)

    def scatter_wait(s):
        pltpu.make_async_copy(obuf.at[s], eo_hbm.at[pl.ds(0, tile * rt)], ssem.at[s]).wait()

    def live(i):
        return (i < nt) & (nv_ref[jnp.minimum(i, nt - 1)] > 0)

    @pl.when(t == 0)
    def _():
        obuf[...] = jnp.zeros_like(obuf)
        for s in range(2):
            cp = pltpu.make_async_copy(obuf.at[s], eo_hbm.at[pl.ds((spare0 + s * tile) * rt, tile * rt)], ssem.at[s])
            cp.start()
            cp.wait()
        cp = src_copy(src0_ref, 0)
        cp.start()
        cp.wait()
        gather(0)

        @pl.when(live(1))
        def _():
            cp = src_copy(src1_ref, 1)
            cp.start()
            cp.wait()

    @pl.when(live(t))
    def _():
        dst_copy().start()

    for s in range(2):
        @pl.when(live(t + 1) & (slot == 1 - s))
        def _(s=s):
            gather(s)

    @pl.when(live(t + 2))
    def _():
        src_copy(srcn_ref, slot).start()

    @pl.when((t == 0) | (te_ref[t] != te_ref[jnp.maximum(t - 1, 0)]))
    def _():
        wgb[...] = wg_ref[...].astype(BF16)
        wub[...] = wu_ref[...].astype(BF16)
        wdb[...] = wd_ref[...].astype(BF16)

    @pl.when(live(t))
    def _():
        pltpu.make_async_copy(x_hbm.at[pl.ds(0, tile * rt)], xbuf.at[slot], gsem.at[slot]).wait()

        @pl.when(t >= 2)
        def _():
            scatter_wait(slot)

        xb = _from_token_tiles(xbuf.at[slot], tile).astype(BF16)
        hg = _mm(xb, wgb[...])
        hu = _mm(xb, wub[...])
        h = (hg * jax.nn.sigmoid(hg) * hu).astype(BF16)
        _to_token_tiles(obuf.at[slot], _mm(h, wdb[...]))
        dst_copy().wait()

        def body(i, carry):
            row = pl.multiple_of(dst_smem[0, i], rt)
            pltpu.make_async_copy(obuf.at[slot, pl.ds(pl.multiple_of(i * rt, rt), rt)], eo_hbm.at[pl.ds(row, rt)],
                                  ssem.at[slot]).start()
            return carry
        lax.fori_loop(0, tile, body, 0, unroll=16)

    @pl.when(live(t + 2))
    def _():
        src_copy(srcn_ref, slot).wait()

    @pl.when(t == nt - 1)
    def _():
        n_live = nlive_ref[0]

        @pl.when(n_live >= 1)
        def _():
            scatter_wait((n_live - 1) % 2)

        @pl.when(n_live >= 2)
        def _():
            scatter_wait(n_live % 2)


def _moe(tile_expert, n_valid, src, dst, x2t, wg, wu, wd, n_tok):
    nt = tile_expert.shape[0]
    tile = MOE_TILE
    rt = ROW_TILES
    n_live = jnp.sum((n_valid > 0).astype(jnp.int32)).reshape(1)
    spare0 = 2 * n_tok
    wspec = lambda shape: pl.BlockSpec((None,) + shape, lambda t, te, nv, nl: (te[t], 0, 0))
    ispec = lambda f: pl.BlockSpec((1, 1, tile), lambda t, te, nv, nl: (f(t), 0, 0))
    grid_spec = pltpu.PrefetchScalarGridSpec(
        num_scalar_prefetch=3,
        grid=(nt,),
        in_specs=[ispec(lambda t: 0), ispec(lambda t: jnp.minimum(t + 1, nt - 1)),
                  ispec(lambda t: jnp.minimum(t + 2, nt - 1)), ispec(lambda t: t),
                  pl.BlockSpec(memory_space=pl.ANY),
                  wspec((D_MODEL, EXPERT_HIDDEN)), wspec((D_MODEL, EXPERT_HIDDEN)),
                  wspec((EXPERT_HIDDEN, D_MODEL))],
        out_specs=pl.BlockSpec(memory_space=pl.ANY),
        scratch_shapes=[pltpu.VMEM((2, tile * rt, LANES), F32),
                        pltpu.VMEM((2, tile * rt, LANES), F32),
                        pltpu.SMEM((2, 1, tile), jnp.int32),
                        pltpu.SMEM((1, tile), jnp.int32),
                        pltpu.SemaphoreType.DMA((2,)),
                        pltpu.SemaphoreType.DMA((2,)),
                        pltpu.SemaphoreType.DMA((2,)),
                        pltpu.SemaphoreType.DMA,
                        pltpu.VMEM((D_MODEL, EXPERT_HIDDEN), BF16),
                        pltpu.VMEM((D_MODEL, EXPERT_HIDDEN), BF16),
                        pltpu.VMEM((EXPERT_HIDDEN, D_MODEL), BF16)])
    return pl.pallas_call(
        functools.partial(_moe_body, nt=nt, spare0=spare0),
        out_shape=jax.ShapeDtypeStruct(((spare0 + 2 * tile) * rt, LANES), F32),
        grid_spec=grid_spec,
        compiler_params=_params(("arbitrary",)),
        name="moe",
    )(tile_expert, n_valid, n_live, src, src, src, dst, x2t, wg, wu, wd)


def _final_body(x_ref, ew_ref, e1_ref, e2_ref, lg_ref, lb_ref, y_ref):
    ew = ew_ref[...]
    n = x_ref.shape[0]
    moe = ew[:, 0:1] * _from_token_tiles(e1_ref, n) + ew[:, 1:2] * _from_token_tiles(e2_ref, n)
    y_ref[...] = _layer_norm(ALPHA * x_ref[...] + moe, lg_ref[...], lb_ref[...])


def _final(x2, ew, eo, lg, lb, row0, t):
    n_tok = x2.shape[0]
    tile = int(np.gcd(np.gcd(row0, t), np.gcd(n_tok, TOKEN_TILE)))
    b0 = row0 // tile
    b1 = n_tok // tile
    const = lambda i: (0, 0)
    blk = lambda off: pl.BlockSpec((tile, D_MODEL), lambda i: (off + i, 0))
    tblk = lambda off: pl.BlockSpec((tile * ROW_TILES, LANES), lambda i: (off + i, 0))
    return pl.pallas_call(
        _final_body,
        out_shape=jax.ShapeDtypeStruct((t, D_MODEL), F32),
        grid=(t // tile,),
        in_specs=[blk(b0), pl.BlockSpec((tile, LANES), lambda i: (b0 + i, 0)), tblk(b0), tblk(b1 + b0),
                  pl.BlockSpec(lg.shape, const), pl.BlockSpec(lb.shape, const)],
        out_specs=blk(0),
        compiler_params=_params(("arbitrary",)),
        name="final",
    )(x2, ew, eo, eo, lg, lb)


def _routing_tables(eid, n_tok):
    tile = MOE_TILE
    n_assign = 2 * n_tok
    assert n_assign % tile == 0
    nt = n_assign // tile + N_EXPERTS
    flat = eid.reshape(-1).astype(jnp.int32)
    experts = jnp.arange(N_EXPERTS, dtype=jnp.int32)
    counts = jnp.sum((flat[:, None] == experts[None, :]).astype(jnp.int32), axis=0)
    pad = (-counts) % tile
    unused = 2 * N_EXPERTS
    pad_keys = jnp.where(jnp.arange(tile, dtype=jnp.int32)[None, :] < pad[:, None],
                         2 * experts[:, None] + 1, unused).reshape(-1)
    keys = jnp.concatenate([2 * flat, pad_keys])
    vals = jnp.concatenate([jnp.arange(n_assign, dtype=jnp.int32),
                            jnp.full((N_EXPERTS * tile,), -1, jnp.int32)])
    keys, vals = lax.sort((keys, vals), num_keys=1)
    keys = keys.reshape(nt, tile)
    vals = vals.reshape(nt, tile)
    tile_expert = jnp.minimum(keys[:, 0] // 2, N_EXPERTS - 1)
    real = vals >= 0
    n_valid = jnp.sum(real.astype(jnp.int32), axis=1)
    a = jnp.maximum(vals, 0)
    src = a >> 1
    spare = 2 * n_tok + (jnp.arange(nt, dtype=jnp.int32)[:, None] % 2) * tile + jnp.arange(tile, dtype=jnp.int32)[None, :]
    dst = jnp.where(real, (a & 1) * n_tok + (a >> 1), spare)
    return tile_expert, n_valid, src.reshape(nt, 1, tile), dst.reshape(nt, 1, tile)


def kernel(x_prompt, x_sample, mem_prompt, cache_dil_k, cache_dil_v, state_gla, cache_mem_k, cache_mem_v, w_in, w_gate_lr, b_gate, g_gla_norm, w_out, ln_mix_g, ln_mix_b, w_mem_q, w_mem_k, w_mem_v, w_mem_o, ln_mem_g, ln_mem_b, w_route_group, b_route_group, w_route_expert, b_route_expert, w_exp_gate, w_exp_up, w_exp_down, ln_ffn_g, ln_ffn_b):
    assert w_in.shape[0] == DEPTH
    bp, seq, _ = x_prompt.shape
    db, tdec, _ = x_sample.shape
    buf = cache_dil_k.shape[2]
    mt = mem_prompt.shape[1]
    n_p, n_s = bp * seq, db * tdec
    n_tok = n_p + n_s
    l = 0

    a0 = 2 * GLA_KEY_W + 2 * GLA_VAL_W
    w_main = jnp.concatenate([w_in[l][:, :a0], w_in[l][:, a0 + GATE_RANK:]], axis=1).astype(BF16)
    w_a = jnp.pad(w_in[l][:, a0:a0 + GATE_RANK], ((0, 0), (0, LANES - GATE_RANK))).astype(BF16)
    w_gl = jnp.pad(w_gate_lr[l], ((0, LANES - GATE_RANK), (0, 0))).astype(BF16)
    b_g = b_gate[l][None, :]
    row1 = lambda a: a[l][None, :]
    w_o = w_out[l].astype(BF16)
    w_q = w_mem_q[l].astype(BF16)
    w_k = w_mem_k[l].astype(BF16)
    w_v = w_mem_v[l].astype(BF16)
    w_mo = w_mem_o[l].astype(BF16)
    w_r = jnp.pad(jnp.concatenate([w_route_expert[l], w_route_group[l]], axis=1),
                  ((0, 0), (0, LANES - N_EXPERTS - N_GROUPS))).astype(BF16)
    b_r = jnp.pad(jnp.concatenate([b_route_expert[l], b_route_group[l]]), (0, LANES - N_EXPERTS - N_GROUPS))[None, :]

    def mixer_inputs(x2d):
        return _proj(x2d, w_main, w_a, w_gl, b_g)

    def post_a(x2d, o_g, r_g, o_d):
        return _post_a(x2d, o_g, r_g, o_d, row1(g_gla_norm), w_o, row1(ln_mix_g), row1(ln_mix_b), w_q)

    xp = x_prompt.reshape(n_p, D_MODEL)
    qg, kg, vg, rg, la, qd, kd, vd = mixer_inputs(xp)
    sh = lambda t: t.reshape(bp, seq, t.shape[-1])
    s0 = jnp.zeros((bp, H_GLA, GLA_DK, GLA_DV), F32)
    o_g, s_fin_p = _gla(sh(qg), sh(kg), sh(vg), sh(la), s0, bb=1, nchunk=4)
    o_d = _dil_prompt(sh(qd), sh(kd), sh(vd))
    mem_k, mem_v = _mem_kv(mem_prompt, w_k, w_v)
    x1_p, qm = post_a(xp, o_g.reshape(n_p, -1), rg, o_d.reshape(n_p, -1))
    tq = min(TOKEN_TILE, seq)
    steps_per_b = seq // tq
    ao_p = _mem_attn(qm.reshape(n_p // tq, tq, D_MODEL), mem_k, mem_v, g=1,
                     kv_of_step=lambda i: i // steps_per_b).reshape(n_p, D_MODEL)
    keep = min(WINDOW_MAX, seq)
    dk_p = sh(kd)[:, seq - keep:].reshape(1, bp, keep, H_DIL, DIL_DH)
    dv_p = sh(vd)[:, seq - keep:].reshape(1, bp, keep, H_DIL, DIL_DH)

    xs = x_sample.reshape(n_s, D_MODEL)
    qg, kg, vg, rg, la, qd, kd, vd = mixer_inputs(xs)
    shs = lambda t: t.reshape(db, tdec, t.shape[-1])
    gb = 8 if db % 8 == 0 else 1
    o_g, s_new = _gla(shs(qg), shs(kg), shs(vg), shs(la), state_gla[l], bb=gb, nchunk=1)
    head_rows = lambda t: t.reshape(db, tdec * H_DIL, DIL_DH)
    groups = lambda c: c[l].reshape(db, buf // DIL_GROUP, DIL_GROUP * H_DIL, DIL_DH)
    nk, nv, o_d = _dil_sample(head_rows(qd), head_rows(kd), head_rows(vd),
                              groups(cache_dil_k), groups(cache_dil_v))
    x1_s, qm = post_a(xs, o_g.reshape(n_s, -1), rg, o_d.reshape(n_s, DIL_W))
    gm = 8 if db % 8 == 0 else 1

    def cache_rows(c):
        return (c[l].reshape(db, mt, MEM_HEADS, MEM_HALVES, LANES).transpose(0, 1, 3, 2, 4)
                .reshape(db, mt * MEM_HALVES * MEM_HEADS, LANES))

    q_rows = (qm.reshape(db, tdec, MEM_HEADS, MEM_HALVES, LANES).transpose(0, 3, 1, 2, 4)
              .reshape(db, MEM_HALVES * tdec * MEM_HEADS, LANES))
    ao_s = _mem_attn_rows(q_rows, cache_rows(cache_mem_k), cache_rows(cache_mem_v), g=gm)
    ao_s = (ao_s.reshape(db, MEM_HALVES, tdec, MEM_HEADS, LANES).transpose(0, 2, 3, 1, 4).reshape(n_s, D_MODEL))

    x2, x2t, eid, ew = _post_c(x1_p, ao_p, x1_s, ao_s, w_mo, row1(ln_mem_g), row1(ln_mem_b), w_r, b_r)
    tile_expert, n_valid, src, dst = _routing_tables(eid[:, :2], n_tok)
    eo = _moe(tile_expert, n_valid, src * ROW_TILES, dst * ROW_TILES, x2t,
              w_exp_gate[l], w_exp_up[l], w_exp_down[l], n_tok)
    y_p = _final(x2, ew, eo, row1(ln_ffn_g), row1(ln_ffn_b), 0, n_p)
    y_s = _final(x2, ew, eo, row1(ln_ffn_g), row1(ln_ffn_b), n_p, n_s)

    return (y_p.reshape(bp, seq, D_MODEL), y_s.reshape(db, tdec, D_MODEL),
            dk_p, dv_p, s_fin_p[None], mem_k.reshape(1, bp, mt, MEM_HEADS, MEM_DH),
            mem_v.reshape(1, bp, mt, MEM_HEADS, MEM_DH),
            nk.reshape(1, db, buf, H_DIL, DIL_DH), nv.reshape(1, db, buf, H_DIL, DIL_DH), s_new[None])
```

```python
import functools

import numpy as np
import jax
import jax.numpy as jnp
from jax import lax
from jax.experimental import pallas as pl
from jax.experimental.pallas import tpu as pltpu

F32 = jnp.float32
BF16 = jnp.bfloat16

D_MODEL = 1024
H_GLA = 4
GLA_DK = 64
GLA_DV = 128
GLA_KEY_W = H_GLA * GLA_DK
GLA_VAL_W = H_GLA * GLA_DV
GATE_RANK = 16
GATE_TAU = 16.0
GLA_CHUNK = 64
H_DIL = 4
DIL_DH = 128
DIL_W = H_DIL * DIL_DH
DIL_PAIRS = ((128, 1), (512, 4), (2048, 16))
DIL_SPAN = 128
WINDOW_MAX = 2048
MEM_HEADS = 4
MEM_DH = 256
N_GROUPS = 4
EXPERTS_PER_GROUP = 8
N_EXPERTS = N_GROUPS * EXPERTS_PER_GROUP
EXPERT_HIDDEN = 512
DEPTH = 1
ALPHA = (2.0 * DEPTH) ** 0.25
LN_EPS = 1e-5
NEG_INF = -1e30

LANES = 128
VMEM_LIMIT = 56 * 1024 * 1024

TOKEN_TILE = 512
MOE_TILE = 256


def _mm(a, b):
    return jnp.dot(a, b, preferred_element_type=F32)


def _mm_nt(a, b):
    return lax.dot_general(a, b, (((1,), (1,)), ((), ())), preferred_element_type=F32)


def _mm_tn(a, b):
    return lax.dot_general(a, b, (((0,), (0,)), ((), ())), preferred_element_type=F32)


def _layer_norm(v, g, b):
    mu = jnp.mean(v, axis=-1, keepdims=True)
    d = v - mu
    var = jnp.mean(d * d, axis=-1, keepdims=True)
    return d * lax.rsqrt(var + LN_EPS) * g + b


def _params(sem):
    return pltpu.CompilerParams(dimension_semantics=sem, vmem_limit_bytes=VMEM_LIMIT)


ROW_TILES = D_MODEL // LANES


def _to_token_tiles(ref, val, base=0):
    n = val.shape[0]
    for s in range(ROW_TILES):
        ref[pl.ds(base + s, n, stride=ROW_TILES), :] = val[:, s * LANES:(s + 1) * LANES]


def _from_token_tiles(ref, n, base=0):
    return jnp.concatenate([ref[pl.ds(base + s, n, stride=ROW_TILES), :] for s in range(ROW_TILES)], axis=1)


PAIR_TILES = D_MODEL // 2 // LANES


def _pack_pairs(ref, val):
    n = val.shape[0]
    half = D_MODEL // 2
    bits = lambda v: pltpu.bitcast(v.astype(BF16).astype(F32), jnp.uint32)
    words = (bits(val[:, :half]) >> 16) | (bits(val[:, half:]) & jnp.uint32(0xFFFF0000))
    for s in range(PAIR_TILES):
        ref[pl.ds(s, n, stride=PAIR_TILES), :] = words[:, s * LANES:(s + 1) * LANES]


def _unpack_pairs(ref, n):
    words = [ref[pl.ds(s, n, stride=PAIR_TILES), :] for s in range(PAIR_TILES)]
    lo = [pltpu.bitcast(w << 16, F32).astype(BF16) for w in words]
    hi = [pltpu.bitcast(w & jnp.uint32(0xFFFF0000), F32).astype(BF16) for w in words]
    return jnp.concatenate(lo + hi, axis=1)


def _proj_body(x_ref, w_ref, wa_ref, wgl_ref, bg_ref,
               qg_o, kg_o, vg_o, r_o, la_o, qd_o, kd_o, vd_o):
    xb = x_ref[...].astype(BF16)

    def mm(lo, hi):
        return _mm(xb, w_ref[:, lo:hi])

    qg_o[...] = mm(0, 256) * (GLA_DK ** -0.5)
    kg_o[...] = mm(256, 512)
    vg_o[...] = mm(512, 1024)
    r_o[...] = mm(1024, 1536)
    qd_o[...] = mm(1536, 2048)
    kd_o[...] = mm(2048, 2560)
    vd_o[...] = mm(2560, 3072)
    a_lr = _mm(xb, wa_ref[...])
    z = _mm(a_lr.astype(BF16), wgl_ref[...]) + bg_ref[...]
    la_o[...] = (jnp.minimum(z, 0.0) - jnp.log1p(jnp.exp(-jnp.abs(z)))) * (1.0 / GATE_TAU)


def _proj(x, w_main, w_a, w_gl, b_g):
    t = x.shape[0]
    tm = min(TOKEN_TILE, t)
    widths = (256, 256, 512, 512, 256, 512, 512, 512)
    row = lambda i: (i, 0)
    const = lambda i: (0, 0)
    return pl.pallas_call(
        _proj_body,
        out_shape=[jax.ShapeDtypeStruct((t, w), F32) for w in widths],
        grid=(t // tm,),
        in_specs=[pl.BlockSpec((tm, D_MODEL), row),
                  pl.BlockSpec(w_main.shape, const),
                  pl.BlockSpec(w_a.shape, const),
                  pl.BlockSpec(w_gl.shape, const),
                  pl.BlockSpec(b_g.shape, const)],
        out_specs=[pl.BlockSpec((tm, w), row) for w in widths],
        compiler_params=_params(("arbitrary",)),
        name="proj",
    )(x, w_main, w_a, w_gl, b_g)


def _split3(a):
    a1 = a.astype(BF16)
    r1 = a - a1.astype(F32)
    a2 = r1.astype(BF16)
    r2 = r1 - a2.astype(F32)
    return a1, a2, r2.astype(BF16)


def _gla_body(q_ref, k_ref, v_ref, la_ref, s0_ref, o_ref, sfin_ref, s_scr, *, bb, rows, nchunk):
    c = GLA_CHUNK
    j = pl.program_id(1)

    @pl.when(j == 0)
    def _():
        s_scr[...] = s0_ref[...]

    ri = lax.broadcasted_iota(jnp.int32, (c, c), 0)
    ci = lax.broadcasted_iota(jnp.int32, (c, c), 1)
    causal = ci <= ri
    tri = jnp.where(causal, 1.0, 0.0).astype(BF16)
    ones = jnp.ones((c, GLA_DV), BF16)

    def pad(t):
        if rows == c * nchunk:
            return t
        return jnp.concatenate([t, jnp.zeros((c * nchunk - rows, t.shape[1]), t.dtype)], axis=0)

    for ib in range(bb):
        q_all = pad(q_ref[ib])
        k_all = pad(k_ref[ib])
        v_all = pad(v_ref[ib])
        la_all = pad(la_ref[ib])
        for ch in range(nchunk):
            sl = slice(ch * c, (ch + 1) * c)
            a = la_all[sl]
            p1, p2, p3 = _split3(a)
            cum = _mm(tri, jnp.concatenate([p1, p2, p3], axis=1))
            b = cum[:, :GLA_KEY_W] + cum[:, GLA_KEY_W:2 * GLA_KEY_W] + cum[:, 2 * GLA_KEY_W:]
            outs = []
            for h in range(H_GLA):
                hk = slice(h * GLA_DK, (h + 1) * GLA_DK)
                hv = slice(h * GLA_DV, (h + 1) * GLA_DV)
                bh = b[:, hk]
                qh = q_all[sl, hk]
                kh = k_all[sl, hk]
                vh = v_all[sl, hv].astype(BF16)
                b_last = bh[c - 1:c, :]
                q_t = (qh * jnp.exp(bh)).astype(BF16)
                k_t = (kh * jnp.exp(-bh)).astype(BF16)
                k_end = (kh * jnp.exp(b_last - bh)).astype(BF16)
                a_in = jnp.where(causal, _mm_nt(q_t, k_t), 0.0).astype(BF16)
                s = s_scr[ib, h]
                outs.append(_mm(a_in, vh) + _mm(q_t, s.astype(BF16)))
                pieces = jnp.concatenate([p1[:, hk], p2[:, hk], p3[:, hk]], axis=1)
                bl = _mm_tn(pieces, ones)
                bl = bl[:GLA_DK] + bl[GLA_DK:2 * GLA_DK] + bl[2 * GLA_DK:]
                s_scr[ib, h] = jnp.exp(bl) * s + _mm_tn(k_end, vh)
            o_chunk = jnp.concatenate(outs, axis=1)
            if rows == c * nchunk:
                o_ref[ib, sl, :] = o_chunk
            else:
                o_ref[ib] = o_chunk[:rows]

    @pl.when(j == pl.num_programs(1) - 1)
    def _():
        sfin_ref[...] = s_scr[...]


def _gla(q, k, v, la, s0, *, bb, nchunk):
    nb, s, _ = q.shape
    rows = min(s, GLA_CHUNK * nchunk)
    assert s % rows == 0 and nb % bb == 0
    blk = lambda i, j: (i, j, 0)
    st = lambda i, j: (i, 0, 0, 0)
    body = functools.partial(_gla_body, bb=bb, rows=rows, nchunk=nchunk)
    return pl.pallas_call(
        body,
        out_shape=[jax.ShapeDtypeStruct((nb, s, GLA_VAL_W), F32),
                   jax.ShapeDtypeStruct((nb, H_GLA, GLA_DK, GLA_DV), F32)],
        grid=(nb // bb, s // rows),
        in_specs=[pl.BlockSpec((bb, rows, GLA_KEY_W), blk),
                  pl.BlockSpec((bb, rows, GLA_KEY_W), blk),
                  pl.BlockSpec((bb, rows, GLA_VAL_W), blk),
                  pl.BlockSpec((bb, rows, GLA_KEY_W), blk),
                  pl.BlockSpec((bb, H_GLA, GLA_DK, GLA_DV), st)],
        out_specs=[pl.BlockSpec((bb, rows, GLA_VAL_W), blk),
                   pl.BlockSpec((bb, H_GLA, GLA_DK, GLA_DV), st)],
        scratch_shapes=[pltpu.VMEM((bb, H_GLA, GLA_DK, GLA_DV), F32)],
        compiler_params=_params(("arbitrary", "arbitrary")),
        name="gla",
    )(q, k, v, la, s0)


DIL_QBLK = 128
DIL_TILE = 2048
DIL_UNROLL = 16


def _dil_body(q_ref, k_ref, v_ref, o_ref, acc_ref, m_ref, l_ref, *, tile):
    blk = DIL_QBLK
    t0 = pl.program_id(2) * tile
    qscale = (DIL_DH ** -0.5) * float(np.log2(np.e))
    ri = lax.broadcasted_iota(jnp.int32, (blk, 2 * blk), 0)
    ci = lax.broadcasted_iota(jnp.int32, (blk, 2 * blk), 1)
    band_bias = jnp.where((ci >= ri) & (ci <= ri + DIL_SPAN), 0.0, NEG_INF)
    prev_cols = lax.broadcasted_iota(jnp.int32, (1, 2 * blk), 1) < blk

    def rows(start, dil):
        return pl.ds(start, blk) if dil == 1 else pl.ds(start, blk, stride=dil)

    def block(dil, q0, first, last):
        qr = rows(q0, dil)
        cur0 = t0 + q0
        prev0 = cur0 - dil * blk
        prev_bias = jnp.where(prev_cols & (prev0 < 0), NEG_INF, 0.0)
        prev0 = jnp.maximum(prev0, 0)
        kw = jnp.concatenate([k_ref[rows(prev0, dil), :], k_ref[rows(cur0, dil), :]], axis=0)
        vw = jnp.concatenate([v_ref[rows(prev0, dil), :], v_ref[rows(cur0, dil), :]], axis=0)
        s = _mm_nt((q_ref[qr, :] * qscale).astype(BF16), kw.astype(BF16)) + band_bias + prev_bias
        m_blk = jnp.max(s, axis=-1, keepdims=True)
        p = jnp.exp2(s - m_blk)
        l_blk = jnp.sum(p, axis=-1, keepdims=True)
        acc = _mm(p.astype(BF16), vw.astype(BF16))
        m_new = jnp.broadcast_to(m_blk, (blk, LANES))
        l_new = jnp.broadcast_to(l_blk, (blk, LANES))
        if not first:
            m_old = m_ref[qr, :]
            m_new = jnp.maximum(m_old, m_new)
            a_old = jnp.exp2(m_old - m_new)
            a_blk = jnp.exp2(m_blk - m_new)
            l_new = a_old * l_ref[qr, :] + a_blk * l_blk
            acc = a_old * acc_ref[qr, :] + a_blk * acc
        if last:
            o_ref[qr, :] = acc / l_new
        else:
            acc_ref[qr, :] = acc
            m_ref[qr, :] = m_new
            l_ref[qr, :] = l_new

    dils = sorted((d for _, d in DIL_PAIRS), reverse=True)
    for bi, dil in enumerate(dils):
        per_res = tile // (dil * blk)

        def body(i, carry, dil=dil, per_res=per_res, bi=bi):
            q0 = (i % dil) + (i // dil) * (dil * blk) if dil > 1 else pl.multiple_of(i * blk, blk)
            block(dil, q0, bi == 0, bi == len(dils) - 1)
            return carry

        lax.fori_loop(0, dil * per_res, body, 0, unroll=DIL_UNROLL)


def _dil_prompt(q, k, v):
    bsz, s, _ = q.shape
    tile = min(DIL_TILE, s)
    assert s % tile == 0 and all(tile % (d * DIL_QBLK) == 0 for _, d in DIL_PAIRS)
    assert all(w // d == DIL_SPAN for w, d in DIL_PAIRS)
    qs = pl.BlockSpec((None, tile, DIL_DH), lambda b, h, n: (b, n, h))
    ks = pl.BlockSpec((None, s, DIL_DH), lambda b, h, n: (b, 0, h))
    return pl.pallas_call(
        functools.partial(_dil_body, tile=tile),
        out_shape=jax.ShapeDtypeStruct((bsz, s, DIL_W), F32),
        grid=(bsz, H_DIL, s // tile),
        in_specs=[qs, ks, ks],
        out_specs=qs,
        scratch_shapes=[pltpu.VMEM((tile, DIL_DH), F32), pltpu.VMEM((tile, LANES), F32),
                        pltpu.VMEM((tile, LANES), F32)],
        compiler_params=_params(("arbitrary", "arbitrary", "arbitrary")),
        name="dil_prompt",
    )(q, k, v)


DIL_GROUP = 16
DIL_TAIL = 512


def _dil_sample_counts(buf, t_new):
    ga = (buf - DIL_TAIL) // DIL_GROUP
    pos_all = np.arange(buf)
    sel = (pos_all >= ga * DIL_GROUP) | (pos_all % DIL_GROUP < t_new)
    t = np.arange(t_new)

    def count(pos, real):
        cnt = np.zeros((t_new, pos.size), np.float32)
        for window, dil in DIL_PAIRS:
            delta = buf + t[:, None] - pos[None, :]
            cnt += (delta >= 0) & (delta % dil == 0) & (delta // dil <= window // dil) & real[None, :]
        return cnt

    assert not count(pos_all[~sel], np.ones((~sel).sum(), bool)).any()
    pad_new = LANES // H_DIL
    pos = np.concatenate([pos_all[sel], buf + np.arange(pad_new)])
    real = np.concatenate([np.ones(sel.sum(), bool), np.arange(pad_new) < t_new])
    cnt = count(pos, real)
    same_head = np.eye(H_DIL, dtype=np.float32)
    full = cnt[:, None, :, None] * same_head[None, :, None, :]
    return full.reshape(t_new * H_DIL, pos.size * H_DIL)


def _dil_sample_body(q_ref, kn_ref, vn_ref, ck_ref, cv_ref, cnt_ref, nk_ref, nv_ref, o_ref, *, ga):
    g = ck_ref.shape[0]
    half = ck_ref.shape[1] // 2
    k_new = kn_ref[...]
    v_new = vn_ref[...]
    for c_ref, n_ref, new in ((ck_ref, nk_ref, k_new), (cv_ref, nv_ref, v_new)):
        n_ref[:, 0:half, :] = c_ref[:, half:, :]
        n_ref[0:g - 1, half:, :] = c_ref[1:g, 0:half, :]
        n_ref[g - 1, half:, :] = new

    zpad = jnp.zeros((LANES - half, DIL_DH), F32)

    def keys(c_ref, new):
        main = c_ref[0:ga, 0:half, :].reshape(ga * half, DIL_DH)
        tail = c_ref[ga:g, :, :].reshape((g - ga) * 2 * half, DIL_DH)
        return jnp.concatenate([main, tail, new, zpad], axis=0).astype(BF16)

    s = _mm_nt(q_ref[...].astype(BF16), keys(ck_ref, k_new)) * (DIL_DH ** -0.5)
    cnt = cnt_ref[...]
    s = jnp.where(cnt > 0.0, s, NEG_INF)
    m = jnp.max(s, axis=-1, keepdims=True)
    p = jnp.exp(s - m) * cnt
    den = jnp.sum(p, axis=-1, keepdims=True)
    o_ref[...] = _mm(p.astype(BF16), keys(cv_ref, v_new)) / den


def _dil_sample(q, k_new, v_new, cache_k, cache_v):
    db, rows_new, _ = q.shape
    g, grows = cache_k.shape[1], cache_k.shape[2]
    assert grows == DIL_GROUP * H_DIL and rows_new * 2 == grows
    buf = g * DIL_GROUP
    ga = (buf - DIL_TAIL) // DIL_GROUP
    cnt = jnp.asarray(_dil_sample_counts(buf, rows_new // H_DIL))
    new = pl.BlockSpec((None, rows_new, DIL_DH), lambda b: (b, 0, 0))
    big = pl.BlockSpec((None, g, grows, DIL_DH), lambda b: (b, 0, 0, 0))
    return pl.pallas_call(
        functools.partial(_dil_sample_body, ga=ga),
        out_shape=[jax.ShapeDtypeStruct(cache_k.shape, F32),
                   jax.ShapeDtypeStruct(cache_v.shape, F32),
                   jax.ShapeDtypeStruct((db, rows_new, DIL_DH), F32)],
        grid=(db,),
        in_specs=[new, new, new, big, big, pl.BlockSpec(cnt.shape, lambda b: (0, 0))],
        out_specs=[big, big, new],
        compiler_params=_params(("arbitrary",)),
        name="dil_sample",
    )(q, k_new, v_new, cache_k, cache_v, cnt)


def _post_a_body(x_ref, og_ref, r_ref, od_ref, gg_ref, wo_ref, lg_ref, lb_ref, wq_ref, x1_o, qm_o):
    og = og_ref[...]
    parts = []
    for h in range(H_GLA):
        oh = og[:, h * GLA_DV:(h + 1) * GLA_DV]
        parts.append(oh * lax.rsqrt(jnp.mean(oh * oh, axis=-1, keepdims=True) + LN_EPS))
    r = r_ref[...]
    on = jnp.concatenate(parts, axis=1) * gg_ref[...] * (r * jax.nn.sigmoid(r))
    cat = jnp.concatenate([on.astype(BF16), od_ref[...].astype(BF16)], axis=1)
    mix = _mm(cat, wo_ref[...])
    x1 = _layer_norm(ALPHA * x_ref[...] + mix, lg_ref[...], lb_ref[...])
    x1_o[...] = x1
    qm_o[...] = _mm(x1.astype(BF16), wq_ref[...]).astype(BF16)


def _post_a(x, og, r, od, gg, wo, lg, lb, wq):
    t = x.shape[0]
    tm = min(TOKEN_TILE, t)
    row = lambda i: (i, 0)
    const = lambda i: (0, 0)
    full = lambda a: pl.BlockSpec(a.shape, const)
    return pl.pallas_call(
        _post_a_body,
        out_shape=[jax.ShapeDtypeStruct((t, D_MODEL), F32), jax.ShapeDtypeStruct((t, D_MODEL), BF16)],
        grid=(t // tm,),
        in_specs=[pl.BlockSpec((tm, D_MODEL), row), pl.BlockSpec((tm, GLA_VAL_W), row),
                  pl.BlockSpec((tm, GLA_VAL_W), row), pl.BlockSpec((tm, DIL_W), row),
                  full(gg), full(wo), full(lg), full(lb), full(wq)],
        out_specs=[pl.BlockSpec((tm, D_MODEL), row), pl.BlockSpec((tm, D_MODEL), row)],
        compiler_params=_params(("arbitrary",)),
        name="post_a",
    )(x, og, r, od, gg, wo, lg, lb, wq)


def _mem_kv_body(m_ref, wk_ref, wv_ref, k_o, v_o):
    mb = m_ref[...].astype(BF16)
    k_o[...] = _mm(mb, wk_ref[...])
    v_o[...] = _mm(mb, wv_ref[...])


def _mem_kv(mem, wk, wv):
    bsz, mt, _ = mem.shape
    blk = pl.BlockSpec((None, mt, D_MODEL), lambda b: (b, 0, 0))
    w = pl.BlockSpec((D_MODEL, D_MODEL), lambda b: (0, 0))
    return pl.pallas_call(
        _mem_kv_body,
        out_shape=[jax.ShapeDtypeStruct((bsz, mt, D_MODEL), F32)] * 2,
        grid=(bsz,),
        in_specs=[blk, w, w],
        out_specs=[blk, blk],
        compiler_params=_params(("arbitrary",)),
        name="mem_kv",
    )(mem, wk, wv)


def _mem_attn_body(q_ref, k_ref, v_ref, o_ref, *, g):
    scale = MEM_DH ** -0.5
    for ig in range(g):
        outs = []
        for h in range(MEM_HEADS):
            hs = slice(h * MEM_DH, (h + 1) * MEM_DH)
            s = _mm_nt(q_ref[ig, :, hs], k_ref[ig, :, hs].astype(BF16)) * scale
            m = jnp.max(s, axis=-1, keepdims=True)
            p = jnp.exp(s - m)
            p = p / jnp.sum(p, axis=-1, keepdims=True)
            outs.append(_mm(p.astype(BF16), v_ref[ig, :, hs].astype(BF16)))
        o_ref[ig] = jnp.concatenate(outs, axis=1).astype(BF16)


def _mem_attn(q, mk, mv, *, g, kv_of_step):
    nq, tq, _ = q.shape
    mt = mk.shape[1]
    qs = pl.BlockSpec((g, tq, D_MODEL), lambda i: (i, 0, 0))
    ks = pl.BlockSpec((g, mt, D_MODEL), lambda i: (kv_of_step(i), 0, 0))
    return pl.pallas_call(
        functools.partial(_mem_attn_body, g=g),
        out_shape=jax.ShapeDtypeStruct((nq, tq, D_MODEL), BF16),
        grid=(nq // g,),
        in_specs=[qs, ks, ks],
        out_specs=qs,
        compiler_params=_params(("arbitrary",)),
        name="mem_attn",
    )(q, mk, mv)


MEM_HALVES = MEM_DH // LANES


def _mem_attn_rows_body(q_ref, k_ref, v_ref, mask_ref, o_ref, *, g):
    scale = MEM_DH ** -0.5
    hh = MEM_HEADS
    nq = q_ref.shape[1] // MEM_HALVES
    nk = k_ref.shape[1]
    valid = mask_ref[...] > 0.0
    for ig in range(g):
        part = _mm_nt(q_ref[ig], k_ref[ig].astype(BF16))
        s = part[0:nq] + pltpu.roll(part[nq:2 * nq], nk - hh, 1)
        s = jnp.where(valid, s * scale, NEG_INF)
        m = jnp.max(s, axis=-1, keepdims=True)
        p = jnp.exp(s - m)
        p = (p / jnp.sum(p, axis=-1, keepdims=True)).astype(F32)
        both = jnp.concatenate([p, pltpu.roll(p, hh, 1)], axis=0).astype(BF16)
        o_ref[ig] = _mm(both, v_ref[ig].astype(BF16)).astype(BF16)


def _mem_attn_rows(q, mk, mv, *, g):
    nb, qr, _ = q.shape
    kr = mk.shape[1]
    nq = qr // MEM_HALVES
    col = np.arange(kr)
    mask = ((col[None, :] % MEM_HEADS == np.arange(nq)[:, None] % MEM_HEADS)
            & (col[None, :] % (MEM_HEADS * MEM_HALVES) < MEM_HEADS)).astype(np.float32)
    qs = pl.BlockSpec((g, qr, LANES), lambda i: (i, 0, 0))
    ks = pl.BlockSpec((g, kr, LANES), lambda i: (i, 0, 0))
    return pl.pallas_call(
        functools.partial(_mem_attn_rows_body, g=g),
        out_shape=jax.ShapeDtypeStruct((nb, qr, LANES), BF16),
        grid=(nb // g,),
        in_specs=[qs, ks, ks, pl.BlockSpec(mask.shape, lambda i: (0, 0))],
        out_specs=qs,
        compiler_params=_params(("arbitrary",)),
        name="mem_attn_rows",
    )(q, mk, mv, jnp.asarray(mask))


ROUTE_GROUP_LANE0 = N_EXPERTS


def _post_c_body(x1p_ref, aop_ref, x1s_ref, aos_ref, wo_ref, lg_ref, lb_ref, wr_ref, br_ref,
                 x2_o, x2p_o, eid_o, ew_o, *, prompt_steps):
    def run(x1_ref, ao_ref):
        x2 = _layer_norm(ALPHA * x1_ref[...] + _mm(ao_ref[...], wo_ref[...]), lg_ref[...], lb_ref[...])
        x2_o[...] = x2
        _pack_pairs(x2p_o, x2)
        logits = _mm(x2.astype(BF16), wr_ref[...]) + br_ref[...]
        lane = lax.broadcasted_iota(jnp.int32, logits.shape, 1).astype(F32)
        big = float(LANES)

        def first_argmax(vals, vmax):
            return jnp.min(jnp.where(vals == vmax, lane, big), axis=-1, keepdims=True)

        is_g = (lane >= ROUTE_GROUP_LANE0) & (lane < ROUTE_GROUP_LANE0 + N_GROUPS)
        gl = jnp.where(is_g, logits, NEG_INF)
        gmax = jnp.max(gl, axis=-1, keepdims=True)
        g_sel = first_argmax(gl, gmax) - ROUTE_GROUP_LANE0
        g_w = 1.0 / jnp.sum(jnp.where(is_g, jnp.exp(gl - gmax), 0.0), axis=-1, keepdims=True)
        lo = g_sel * EXPERTS_PER_GROUP
        el = jnp.where((lane >= lo) & (lane < lo + EXPERTS_PER_GROUP), logits, NEG_INF)
        v1 = jnp.max(el, axis=-1, keepdims=True)
        i1 = first_argmax(el, v1)
        el2 = jnp.where(lane == i1, NEG_INF, el)
        v2 = jnp.max(el2, axis=-1, keepdims=True)
        i2 = first_argmax(el2, v2)
        e = jnp.exp(v2 - v1)
        w1 = g_w / (1.0 + e)
        w2 = g_w * e / (1.0 + e)
        eid_o[...] = jnp.where(lane == 0.0, i1, jnp.where(lane == 1.0, i2, 0.0)).astype(jnp.int32)
        ew_o[...] = jnp.where(lane == 0.0, w1, jnp.where(lane == 1.0, w2, 0.0))

    i = pl.program_id(0)

    @pl.when(i < prompt_steps)
    def _():
        run(x1p_ref, aop_ref)

    @pl.when(i >= prompt_steps)
    def _():
        run(x1s_ref, aos_ref)


def _post_c(x1p, aop, x1s, aos, wo, lg, lb, wr, br):
    n_p, n_s = x1p.shape[0], x1s.shape[0]
    tm = int(np.gcd(np.gcd(n_p, n_s), TOKEN_TILE))
    sp, ss = n_p // tm, n_s // tm
    prow = lambda i: (jnp.minimum(i, sp - 1), 0)
    srow = lambda i: (jnp.maximum(i - sp, 0), 0)
    orow = lambda i: (i, 0)
    const = lambda i: (0, 0)
    full = lambda a: pl.BlockSpec(a.shape, const)
    n_total = n_p + n_s
    return pl.pallas_call(
        functools.partial(_post_c_body, prompt_steps=sp),
        out_shape=[jax.ShapeDtypeStruct((n_total, D_MODEL), F32),
                   jax.ShapeDtypeStruct((n_total * PAIR_TILES, LANES), jnp.uint32),
                   jax.ShapeDtypeStruct((n_total, LANES), jnp.int32),
                   jax.ShapeDtypeStruct((n_total, LANES), F32)],
        grid=(sp + ss,),
        in_specs=[pl.BlockSpec((tm, D_MODEL), prow), pl.BlockSpec((tm, D_MODEL), prow),
                  pl.BlockSpec((tm, D_MODEL), srow), pl.BlockSpec((tm, D_MODEL), srow),
                  full(wo), full(lg), full(lb), full(wr), full(br)],
        out_specs=[pl.BlockSpec((tm, D_MODEL), orow), pl.BlockSpec((tm * PAIR_TILES, LANES), orow),
                   pl.BlockSpec((tm, LANES), orow), pl.BlockSpec((tm, LANES), orow)],
        compiler_params=_params(("arbitrary",)),
        name="post_c",
    )(x1p, aop, x1s, aos, wo, lg, lb, wr, br)


def _moe_body(te_ref, nv_ref, nlive_ref, src0_ref, srcn_ref, dst0_ref, dstn_ref, xp_hbm, wg_ref, wu_ref, wd_ref,
              eo_hbm, xp, gbuf, obuf, src_smem, dst_smem, xsem, ssem, isem, dsem, wgb, wub, wdb, *, nt, spare0):
    t = pl.program_id(0)
    tile = MOE_TILE
    slot = t % 2
    rt = ROW_TILES
    pt = PAIR_TILES

    def src_copy(idx_vmem, s):
        return pltpu.make_async_copy(idx_vmem.at[0], src_smem.at[s], isem.at[s])

    def dst_copy(idx_vmem, s):
        return pltpu.make_async_copy(idx_vmem.at[0], dst_smem.at[s], dsem.at[s])

    def scatter_wait(s):
        pltpu.make_async_copy(obuf.at[s], eo_hbm.at[pl.ds(0, tile * rt)], ssem.at[s]).wait()

    def live(i):
        return (i < nt) & (nv_ref[jnp.minimum(i, nt - 1)] > 0)

    @pl.when(t == 0)
    def _():
        xcp = pltpu.make_async_copy(xp_hbm, xp, xsem)
        xcp.start()
        obuf[...] = jnp.zeros_like(obuf)
        for s in range(2):
            cp = pltpu.make_async_copy(obuf.at[s], eo_hbm.at[pl.ds((spare0 + s * tile) * rt, tile * rt)], ssem.at[s])
            cp.start()
            cp.wait()
        for cp in (src_copy(src0_ref, 0), dst_copy(dst0_ref, 0)):
            cp.start()
            cp.wait()
        xcp.wait()

    @pl.when(live(t + 1))
    def _():
        src_copy(srcn_ref, 1 - slot).start()
        dst_copy(dstn_ref, 1 - slot).start()

    @pl.when((t == 0) | (te_ref[t] != te_ref[jnp.maximum(t - 1, 0)]))
    def _():
        wgb[...] = wg_ref[...].astype(BF16)
        wub[...] = wu_ref[...].astype(BF16)
        wdb[...] = wd_ref[...].astype(BF16)

    for s in range(2):
        @pl.when(live(t) & (slot == s))
        def _(s=s):
            def gather(i, carry):
                row = pl.multiple_of(src_smem[s, 0, i], pt)
                gbuf[pl.ds(pl.multiple_of(i * pt, pt), pt), :] = xp[pl.ds(row, pt), :]
                return carry
            lax.fori_loop(0, tile, gather, 0, unroll=16)

            @pl.when(t >= 2)
            def _():
                scatter_wait(s)

            xb = _unpack_pairs(gbuf, tile)
            hg = _mm(xb, wgb[...])
            hu = _mm(xb, wub[...])
            h = (hg * jax.nn.sigmoid(hg) * hu).astype(BF16)
            _to_token_tiles(obuf.at[s], _mm(h, wdb[...]))

            def scatter(c, carry):
                for u in range(MOE_DMA_UNROLL):
                    i = c * MOE_DMA_UNROLL + u
                    row = pl.multiple_of(dst_smem[s, 0, i], rt)
                    pltpu.make_async_copy(obuf.at[s, pl.ds(pl.multiple_of(i * rt, rt), rt)],
                                          eo_hbm.at[pl.ds(row, rt)], ssem.at[s]).start(priority=u % 2)
                return carry
            lax.fori_loop(0, tile // MOE_DMA_UNROLL, scatter, 0)

    @pl.when(live(t + 1))
    def _():
        src_copy(srcn_ref, 1 - slot).wait()
        dst_copy(dstn_ref, 1 - slot).wait()

    @pl.when(t == nt - 1)
    def _():
        n_live = nlive_ref[0]

        @pl.when(n_live >= 1)
        def _():
            scatter_wait((n_live - 1) % 2)

        @pl.when(n_live >= 2)
        def _():
            scatter_wait(n_live % 2)


MOE_DMA_UNROLL = 16
MOE_VMEM_LIMIT = 60 * 1024 * 1024


def _moe(tile_expert, n_valid, src, dst, x2p, wg, wu, wd, n_tok):
    nt = tile_expert.shape[0]
    tile = MOE_TILE
    rt = ROW_TILES
    n_live = jnp.sum((n_valid > 0).astype(jnp.int32)).reshape(1)
    spare0 = 2 * n_tok
    wspec = lambda shape: pl.BlockSpec((None,) + shape, lambda t, te, nv, nl: (te[t], 0, 0))
    ispec = lambda f: pl.BlockSpec((1, 1, tile), lambda t, te, nv, nl: (f(t), 0, 0))
    first = lambda t: 0
    nxt = lambda t: jnp.minimum(t + 1, nt - 1)
    grid_spec = pltpu.PrefetchScalarGridSpec(
        num_scalar_prefetch=3,
        grid=(nt,),
        in_specs=[ispec(first), ispec(nxt), ispec(first), ispec(nxt),
                  pl.BlockSpec(memory_space=pl.ANY),
                  wspec((D_MODEL, EXPERT_HIDDEN)), wspec((D_MODEL, EXPERT_HIDDEN)),
                  wspec((EXPERT_HIDDEN, D_MODEL))],
        out_specs=pl.BlockSpec(memory_space=pl.ANY),
        scratch_shapes=[pltpu.VMEM(x2p.shape, jnp.uint32),
                        pltpu.VMEM((tile * PAIR_TILES, LANES), jnp.uint32),
                        pltpu.VMEM((2, tile * rt, LANES), F32),
                        pltpu.SMEM((2, 1, tile), jnp.int32),
                        pltpu.SMEM((2, 1, tile), jnp.int32),
                        pltpu.SemaphoreType.DMA,
                        pltpu.SemaphoreType.DMA((2,)),
                        pltpu.SemaphoreType.DMA((2,)),
                        pltpu.SemaphoreType.DMA((2,)),
                        pltpu.VMEM((D_MODEL, EXPERT_HIDDEN), BF16),
                        pltpu.VMEM((D_MODEL, EXPERT_HIDDEN), BF16),
                        pltpu.VMEM((EXPERT_HIDDEN, D_MODEL), BF16)])
    return pl.pallas_call(
        functools.partial(_moe_body, nt=nt, spare0=spare0),
        out_shape=jax.ShapeDtypeStruct(((spare0 + 2 * tile) * rt, LANES), F32),
        grid_spec=grid_spec,
        compiler_params=pltpu.CompilerParams(dimension_semantics=("arbitrary",), vmem_limit_bytes=MOE_VMEM_LIMIT),
        name="moe",
    )(tile_expert, n_valid, n_live, src, src, dst, dst, x2p, wg, wu, wd)


def _final_body(x_ref, ew_ref, e1_ref, e2_ref, lg_ref, lb_ref, y_ref):
    ew = ew_ref[...]
    n = x_ref.shape[0]
    moe = ew[:, 0:1] * _from_token_tiles(e1_ref, n) + ew[:, 1:2] * _from_token_tiles(e2_ref, n)
    y_ref[...] = _layer_norm(ALPHA * x_ref[...] + moe, lg_ref[...], lb_ref[...])


def _final(x2, ew, eo, lg, lb, row0, t):
    n_tok = x2.shape[0]
    tile = int(np.gcd(np.gcd(row0, t), np.gcd(n_tok, TOKEN_TILE)))
    b0 = row0 // tile
    b1 = n_tok // tile
    const = lambda i: (0, 0)
    blk = lambda off: pl.BlockSpec((tile, D_MODEL), lambda i: (off + i, 0))
    tblk = lambda off: pl.BlockSpec((tile * ROW_TILES, LANES), lambda i: (off + i, 0))
    return pl.pallas_call(
        _final_body,
        out_shape=jax.ShapeDtypeStruct((t, D_MODEL), F32),
        grid=(t // tile,),
        in_specs=[blk(b0), pl.BlockSpec((tile, LANES), lambda i: (b0 + i, 0)), tblk(b0), tblk(b1 + b0),
                  pl.BlockSpec(lg.shape, const), pl.BlockSpec(lb.shape, const)],
        out_specs=blk(0),
        compiler_params=_params(("arbitrary",)),
        name="final",
    )(x2, ew, eo, eo, lg, lb)


def _routing_tables(eid, n_tok):
    tile = MOE_TILE
    n_assign = 2 * n_tok
    assert n_assign % tile == 0
    nt = n_assign // tile + N_EXPERTS
    flat = eid.reshape(-1).astype(jnp.int32)
    experts = jnp.arange(N_EXPERTS, dtype=jnp.int32)
    counts = jnp.sum((flat[:, None] == experts[None, :]).astype(jnp.int32), axis=0)
    pad = (-counts) % tile
    unused = 2 * N_EXPERTS
    pad_keys = jnp.where(jnp.arange(tile, dtype=jnp.int32)[None, :] < pad[:, None],
                         2 * experts[:, None] + 1, unused).reshape(-1)
    keys = jnp.concatenate([2 * flat, pad_keys])
    vals = jnp.concatenate([jnp.arange(n_assign, dtype=jnp.int32),
                            jnp.full((N_EXPERTS * tile,), -1, jnp.int32)])
    keys, vals = lax.sort((keys, vals), num_keys=1)
    keys = keys.reshape(nt, tile)
    vals = vals.reshape(nt, tile)
    tile_expert = jnp.minimum(keys[:, 0] // 2, N_EXPERTS - 1)
    real = vals >= 0
    n_valid = jnp.sum(real.astype(jnp.int32), axis=1)
    a = jnp.maximum(vals, 0)
    src = a >> 1
    spare = 2 * n_tok + (jnp.arange(nt, dtype=jnp.int32)[:, None] % 2) * tile + jnp.arange(tile, dtype=jnp.int32)[None, :]
    dst = jnp.where(real, (a & 1) * n_tok + (a >> 1), spare)
    return tile_expert, n_valid, src.reshape(nt, 1, tile), dst.reshape(nt, 1, tile)


def kernel(x_prompt, x_sample, mem_prompt, cache_dil_k, cache_dil_v, state_gla, cache_mem_k, cache_mem_v, w_in, w_gate_lr, b_gate, g_gla_norm, w_out, ln_mix_g, ln_mix_b, w_mem_q, w_mem_k, w_mem_v, w_mem_o, ln_mem_g, ln_mem_b, w_route_group, b_route_group, w_route_expert, b_route_expert, w_exp_gate, w_exp_up, w_exp_down, ln_ffn_g, ln_ffn_b):
    assert w_in.shape[0] == DEPTH
    bp, seq, _ = x_prompt.shape
    db, tdec, _ = x_sample.shape
    buf = cache_dil_k.shape[2]
    mt = mem_prompt.shape[1]
    n_p, n_s = bp * seq, db * tdec
    n_tok = n_p + n_s
    l = 0

    a0 = 2 * GLA_KEY_W + 2 * GLA_VAL_W
    w_main = jnp.concatenate([w_in[l][:, :a0], w_in[l][:, a0 + GATE_RANK:]], axis=1).astype(BF16)
    w_a = jnp.pad(w_in[l][:, a0:a0 + GATE_RANK], ((0, 0), (0, LANES - GATE_RANK))).astype(BF16)
    w_gl = jnp.pad(w_gate_lr[l], ((0, LANES - GATE_RANK), (0, 0))).astype(BF16)
    b_g = b_gate[l][None, :]
    row1 = lambda a: a[l][None, :]
    w_o = w_out[l].astype(BF16)
    w_q = w_mem_q[l].astype(BF16)
    w_k = w_mem_k[l].astype(BF16)
    w_v = w_mem_v[l].astype(BF16)
    w_mo = w_mem_o[l].astype(BF16)
    w_r = jnp.pad(jnp.concatenate([w_route_expert[l], w_route_group[l]], axis=1),
                  ((0, 0), (0, LANES - N_EXPERTS - N_GROUPS))).astype(BF16)
    b_r = jnp.pad(jnp.concatenate([b_route_expert[l], b_route_group[l]]), (0, LANES - N_EXPERTS - N_GROUPS))[None, :]

    def mixer_inputs(x2d):
        return _proj(x2d, w_main, w_a, w_gl, b_g)

    def post_a(x2d, o_g, r_g, o_d):
        return _post_a(x2d, o_g, r_g, o_d, row1(g_gla_norm), w_o, row1(ln_mix_g), row1(ln_mix_b), w_q)

    xp = x_prompt.reshape(n_p, D_MODEL)
    qg, kg, vg, rg, la, qd, kd, vd = mixer_inputs(xp)
    sh = lambda t: t.reshape(bp, seq, t.shape[-1])
    s0 = jnp.zeros((bp, H_GLA, GLA_DK, GLA_DV), F32)
    o_g, s_fin_p = _gla(sh(qg), sh(kg), sh(vg), sh(la), s0, bb=1, nchunk=4)
    o_d = _dil_prompt(sh(qd), sh(kd), sh(vd))
    mem_k, mem_v = _mem_kv(mem_prompt, w_k, w_v)
    x1_p, qm = post_a(xp, o_g.reshape(n_p, -1), rg, o_d.reshape(n_p, -1))
    tq = min(TOKEN_TILE, seq)
    steps_per_b = seq // tq
    ao_p = _mem_attn(qm.reshape(n_p // tq, tq, D_MODEL), mem_k, mem_v, g=1,
                     kv_of_step=lambda i: i // steps_per_b).reshape(n_p, D_MODEL)
    keep = min(WINDOW_MAX, seq)
    dk_p = sh(kd)[:, seq - keep:].reshape(1, bp, keep, H_DIL, DIL_DH)
    dv_p = sh(vd)[:, seq - keep:].reshape(1, bp, keep, H_DIL, DIL_DH)

    xs = x_sample.reshape(n_s, D_MODEL)
    qg, kg, vg, rg, la, qd, kd, vd = mixer_inputs(xs)
    shs = lambda t: t.reshape(db, tdec, t.shape[-1])
    gb = 8 if db % 8 == 0 else 1
    o_g, s_new = _gla(shs(qg), shs(kg), shs(vg), shs(la), state_gla[l], bb=gb, nchunk=1)
    head_rows = lambda t: t.reshape(db, tdec * H_DIL, DIL_DH)
    groups = lambda c: c[l].reshape(db, buf // DIL_GROUP, DIL_GROUP * H_DIL, DIL_DH)
    nk, nv, o_d = _dil_sample(head_rows(qd), head_rows(kd), head_rows(vd),
                              groups(cache_dil_k), groups(cache_dil_v))
    x1_s, qm = post_a(xs, o_g.reshape(n_s, -1), rg, o_d.reshape(n_s, DIL_W))
    gm = 8 if db % 8 == 0 else 1

    def cache_rows(c):
        return (c[l].reshape(db, mt, MEM_HEADS, MEM_HALVES, LANES).transpose(0, 1, 3, 2, 4)
                .reshape(db, mt * MEM_HALVES * MEM_HEADS, LANES))

    q_rows = (qm.reshape(db, tdec, MEM_HEADS, MEM_HALVES, LANES).transpose(0, 3, 1, 2, 4)
              .reshape(db, MEM_HALVES * tdec * MEM_HEADS, LANES))
    ao_s = _mem_attn_rows(q_rows, cache_rows(cache_mem_k), cache_rows(cache_mem_v), g=gm)
    ao_s = (ao_s.reshape(db, MEM_HALVES, tdec, MEM_HEADS, LANES).transpose(0, 2, 3, 1, 4).reshape(n_s, D_MODEL))

    x2, x2p, eid, ew = _post_c(x1_p, ao_p, x1_s, ao_s, w_mo, row1(ln_mem_g), row1(ln_mem_b), w_r, b_r)
    tile_expert, n_valid, src, dst = _routing_tables(eid[:, :2], n_tok)
    eo = _moe(tile_expert, n_valid, src * PAIR_TILES, dst * ROW_TILES, x2p,
              w_exp_gate[l], w_exp_up[l], w_exp_down[l], n_tok)
    y_p = _final(x2, ew, eo, row1(ln_ffn_g), row1(ln_ffn_b), 0, n_p)
    y_s = _final(x2, ew, eo, row1(ln_ffn_g), row1(ln_ffn_b), n_p, n_s)

    return (y_p.reshape(bp, seq, D_MODEL), y_s.reshape(db, tdec, D_MODEL),
            dk_p, dv_p, s_fin_p[None], mem_k.reshape(1, bp, mt, MEM_HEADS, MEM_DH),
            mem_v.reshape(1, bp, mt, MEM_HEADS, MEM_DH),
            nk.reshape(1, db, buf, H_DIL, DIL_DH), nv.reshape(1, db, buf, H_DIL, DIL_DH), s_new[None])
```

```python
import functools

import numpy as np
import jax
import jax.numpy as jnp
from jax import lax
from jax.experimental import pallas as pl
from jax.experimental.pallas import tpu as pltpu

F32 = jnp.float32
BF16 = jnp.bfloat16

D_MODEL = 1024
H_GLA = 4
GLA_DK = 64
GLA_DV = 128
GLA_KEY_W = H_GLA * GLA_DK
GLA_VAL_W = H_GLA * GLA_DV
GATE_RANK = 16
GATE_TAU = 16.0
GLA_CHUNK = 64
H_DIL = 4
DIL_DH = 128
DIL_W = H_DIL * DIL_DH
DIL_PAIRS = ((128, 1), (512, 4), (2048, 16))
DIL_SPAN = 128
WINDOW_MAX = 2048
MEM_HEADS = 4
MEM_DH = 256
N_GROUPS = 4
EXPERTS_PER_GROUP = 8
N_EXPERTS = N_GROUPS * EXPERTS_PER_GROUP
EXPERT_HIDDEN = 512
DEPTH = 1
ALPHA = (2.0 * DEPTH) ** 0.25
LN_EPS = 1e-5
NEG_INF = -1e30

LANES = 128
VMEM_LIMIT = 56 * 1024 * 1024

TOKEN_TILE = 512
MOE_TILE = 256


def _mm(a, b):
    return jnp.dot(a, b, preferred_element_type=F32)


def _mm_nt(a, b):
    return lax.dot_general(a, b, (((1,), (1,)), ((), ())), preferred_element_type=F32)


def _mm_tn(a, b):
    return lax.dot_general(a, b, (((0,), (0,)), ((), ())), preferred_element_type=F32)


def _layer_norm(v, g, b):
    mu = jnp.mean(v, axis=-1, keepdims=True)
    d = v - mu
    var = jnp.mean(d * d, axis=-1, keepdims=True)
    return d * lax.rsqrt(var + LN_EPS) * g + b


def _params(sem):
    return pltpu.CompilerParams(dimension_semantics=sem, vmem_limit_bytes=VMEM_LIMIT)


ROW_TILES = D_MODEL // LANES


def _to_token_tiles(ref, val, col0=0):
    n = val.shape[0]
    for s in range(val.shape[1] // LANES):
        ref[pl.ds(col0 // LANES + s, n, stride=ROW_TILES), :] = val[:, s * LANES:(s + 1) * LANES]


def _from_token_tiles(ref, n):
    return jnp.concatenate([ref[pl.ds(s, n, stride=ROW_TILES), :] for s in range(ROW_TILES)], axis=1)


PAIR_TILES = D_MODEL // 2 // LANES


def _pack_pairs(ref, val):
    n = val.shape[0]
    half = D_MODEL // 2
    bits = lambda v: pltpu.bitcast(v.astype(BF16).astype(F32), jnp.uint32)
    words = (bits(val[:, :half]) >> 16) | (bits(val[:, half:]) & jnp.uint32(0xFFFF0000))
    for s in range(PAIR_TILES):
        ref[pl.ds(s, n, stride=PAIR_TILES), :] = words[:, s * LANES:(s + 1) * LANES]


def _unpack_pairs(ref, n):
    words = [ref[pl.ds(s, n, stride=PAIR_TILES), :] for s in range(PAIR_TILES)]
    lo = [pltpu.bitcast(w << 16, F32).astype(BF16) for w in words]
    hi = [pltpu.bitcast(w & jnp.uint32(0xFFFF0000), F32).astype(BF16) for w in words]
    return jnp.concatenate(lo + hi, axis=1)


def _proj_body(x_ref, w_ref, wa_ref, wgl_ref, bg_ref,
               qg_o, kg_o, vg_o, r_o, la_o, qd_o, kd_o, vd_o):
    xb = x_ref[...].astype(BF16)

    def mm(lo, hi):
        return _mm(xb, w_ref[:, lo:hi])

    qg_o[...] = mm(0, 256) * (GLA_DK ** -0.5)
    kg_o[...] = mm(256, 512)
    vg_o[...] = mm(512, 1024)
    r_o[...] = mm(1024, 1536)
    qd_o[...] = mm(1536, 2048)
    kd_o[...] = mm(2048, 2560)
    vd_o[...] = mm(2560, 3072)
    a_lr = _mm(xb, wa_ref[...])
    z = _mm(a_lr.astype(BF16), wgl_ref[...]) + bg_ref[...]
    la_o[...] = (jnp.minimum(z, 0.0) - jnp.log1p(jnp.exp(-jnp.abs(z)))) * (1.0 / GATE_TAU)


def _proj(x, w_main, w_a, w_gl, b_g):
    t = x.shape[0]
    tm = min(TOKEN_TILE, t)
    widths = (256, 256, 512, 512, 256, 512, 512, 512)
    row = lambda i: (i, 0)
    const = lambda i: (0, 0)
    return pl.pallas_call(
        _proj_body,
        out_shape=[jax.ShapeDtypeStruct((t, w), F32) for w in widths],
        grid=(t // tm,),
        in_specs=[pl.BlockSpec((tm, D_MODEL), row),
                  pl.BlockSpec(w_main.shape, const),
                  pl.BlockSpec(w_a.shape, const),
                  pl.BlockSpec(w_gl.shape, const),
                  pl.BlockSpec(b_g.shape, const)],
        out_specs=[pl.BlockSpec((tm, w), row) for w in widths],
        compiler_params=_params(("arbitrary",)),
        name="proj",
    )(x, w_main, w_a, w_gl, b_g)


def _split3(a):
    a1 = a.astype(BF16)
    r1 = a - a1.astype(F32)
    a2 = r1.astype(BF16)
    r2 = r1 - a2.astype(F32)
    return a1, a2, r2.astype(BF16)


def _gla_body(q_ref, k_ref, v_ref, la_ref, s0_ref, o_ref, sfin_ref, s_scr, *, bb, rows, nchunk):
    c = GLA_CHUNK
    j = pl.program_id(1)

    @pl.when(j == 0)
    def _():
        s_scr[...] = s0_ref[...]

    ri = lax.broadcasted_iota(jnp.int32, (c, c), 0)
    ci = lax.broadcasted_iota(jnp.int32, (c, c), 1)
    causal = ci <= ri
    tri = jnp.where(causal, 1.0, 0.0).astype(BF16)
    ones = jnp.ones((c, GLA_DV), BF16)

    def pad(t):
        if rows == c * nchunk:
            return t
        return jnp.concatenate([t, jnp.zeros((c * nchunk - rows, t.shape[1]), t.dtype)], axis=0)

    for ib in range(bb):
        q_all = pad(q_ref[ib])
        k_all = pad(k_ref[ib])
        v_all = pad(v_ref[ib])
        la_all = pad(la_ref[ib])
        for ch in range(nchunk):
            sl = slice(ch * c, (ch + 1) * c)
            a = la_all[sl]
            p1, p2, p3 = _split3(a)
            cum = _mm(tri, jnp.concatenate([p1, p2, p3], axis=1))
            b = cum[:, :GLA_KEY_W] + cum[:, GLA_KEY_W:2 * GLA_KEY_W] + cum[:, 2 * GLA_KEY_W:]
            outs = []
            for h in range(H_GLA):
                hk = slice(h * GLA_DK, (h + 1) * GLA_DK)
                hv = slice(h * GLA_DV, (h + 1) * GLA_DV)
                bh = b[:, hk]
                qh = q_all[sl, hk]
                kh = k_all[sl, hk]
                vh = v_all[sl, hv].astype(BF16)
                b_last = bh[c - 1:c, :]
                q_t = (qh * jnp.exp(bh)).astype(BF16)
                k_t = (kh * jnp.exp(-bh)).astype(BF16)
                k_end = (kh * jnp.exp(b_last - bh)).astype(BF16)
                a_in = jnp.where(causal, _mm_nt(q_t, k_t), 0.0).astype(BF16)
                s = s_scr[ib, h]
                outs.append(_mm(a_in, vh) + _mm(q_t, s.astype(BF16)))
                pieces = jnp.concatenate([p1[:, hk], p2[:, hk], p3[:, hk]], axis=1)
                bl = _mm_tn(pieces, ones)
                bl = bl[:GLA_DK] + bl[GLA_DK:2 * GLA_DK] + bl[2 * GLA_DK:]
                s_scr[ib, h] = jnp.exp(bl) * s + _mm_tn(k_end, vh)
            o_chunk = jnp.concatenate(outs, axis=1)
            if rows == c * nchunk:
                o_ref[ib, sl, :] = o_chunk
            else:
                o_ref[ib] = o_chunk[:rows]

    @pl.when(j == pl.num_programs(1) - 1)
    def _():
        sfin_ref[...] = s_scr[...]


def _gla(q, k, v, la, s0, *, bb, nchunk):
    nb, s, _ = q.shape
    rows = min(s, GLA_CHUNK * nchunk)
    assert s % rows == 0 and nb % bb == 0
    blk = lambda i, j: (i, j, 0)
    st = lambda i, j: (i, 0, 0, 0)
    body = functools.partial(_gla_body, bb=bb, rows=rows, nchunk=nchunk)
    return pl.pallas_call(
        body,
        out_shape=[jax.ShapeDtypeStruct((nb, s, GLA_VAL_W), F32),
                   jax.ShapeDtypeStruct((nb, H_GLA, GLA_DK, GLA_DV), F32)],
        grid=(nb // bb, s // rows),
        in_specs=[pl.BlockSpec((bb, rows, GLA_KEY_W), blk),
                  pl.BlockSpec((bb, rows, GLA_KEY_W), blk),
                  pl.BlockSpec((bb, rows, GLA_VAL_W), blk),
                  pl.BlockSpec((bb, rows, GLA_KEY_W), blk),
                  pl.BlockSpec((bb, H_GLA, GLA_DK, GLA_DV), st)],
        out_specs=[pl.BlockSpec((bb, rows, GLA_VAL_W), blk),
                   pl.BlockSpec((bb, H_GLA, GLA_DK, GLA_DV), st)],
        scratch_shapes=[pltpu.VMEM((bb, H_GLA, GLA_DK, GLA_DV), F32)],
        compiler_params=_params(("arbitrary", "arbitrary")),
        name="gla",
    )(q, k, v, la, s0)


DIL_QBLK = 128
DIL_TILE = 2048
DIL_UNROLL = 16


def _dil_body(q_ref, k_ref, v_ref, o_ref, acc_ref, m_ref, l_ref, *, tile):
    blk = DIL_QBLK
    t0 = pl.program_id(2) * tile
    qscale = (DIL_DH ** -0.5) * float(np.log2(np.e))
    ri = lax.broadcasted_iota(jnp.int32, (blk, 2 * blk), 0)
    ci = lax.broadcasted_iota(jnp.int32, (blk, 2 * blk), 1)
    band_bias = jnp.where((ci >= ri) & (ci <= ri + DIL_SPAN), 0.0, NEG_INF)
    prev_cols = lax.broadcasted_iota(jnp.int32, (1, 2 * blk), 1) < blk

    def rows(start, dil):
        return pl.ds(start, blk) if dil == 1 else pl.ds(start, blk, stride=dil)

    def block(dil, q0, first, last):
        qr = rows(q0, dil)
        cur0 = t0 + q0
        prev0 = cur0 - dil * blk
        prev_bias = jnp.where(prev_cols & (prev0 < 0), NEG_INF, 0.0)
        prev0 = jnp.maximum(prev0, 0)
        kw = jnp.concatenate([k_ref[rows(prev0, dil), :], k_ref[rows(cur0, dil), :]], axis=0)
        vw = jnp.concatenate([v_ref[rows(prev0, dil), :], v_ref[rows(cur0, dil), :]], axis=0)
        s = _mm_nt((q_ref[qr, :] * qscale).astype(BF16), kw.astype(BF16)) + band_bias + prev_bias
        m_blk = jnp.max(s, axis=-1, keepdims=True)
        p = jnp.exp2(s - m_blk)
        l_blk = jnp.sum(p, axis=-1, keepdims=True)
        acc = _mm(p.astype(BF16), vw.astype(BF16))
        m_new = jnp.broadcast_to(m_blk, (blk, LANES))
        l_new = jnp.broadcast_to(l_blk, (blk, LANES))
        if not first:
            m_old = m_ref[qr, :]
            m_new = jnp.maximum(m_old, m_new)
            a_old = jnp.exp2(m_old - m_new)
            a_blk = jnp.exp2(m_blk - m_new)
            l_new = a_old * l_ref[qr, :] + a_blk * l_blk
            acc = a_old * acc_ref[qr, :] + a_blk * acc
        if last:
            o_ref[qr, :] = acc / l_new
        else:
            acc_ref[qr, :] = acc
            m_ref[qr, :] = m_new
            l_ref[qr, :] = l_new

    dils = sorted((d for _, d in DIL_PAIRS), reverse=True)
    for bi, dil in enumerate(dils):
        per_res = tile // (dil * blk)

        def body(i, carry, dil=dil, per_res=per_res, bi=bi):
            q0 = (i % dil) + (i // dil) * (dil * blk) if dil > 1 else pl.multiple_of(i * blk, blk)
            block(dil, q0, bi == 0, bi == len(dils) - 1)
            return carry

        lax.fori_loop(0, dil * per_res, body, 0, unroll=DIL_UNROLL)


def _dil_prompt(q, k, v):
    bsz, s, _ = q.shape
    tile = min(DIL_TILE, s)
    assert s % tile == 0 and all(tile % (d * DIL_QBLK) == 0 for _, d in DIL_PAIRS)
    assert all(w // d == DIL_SPAN for w, d in DIL_PAIRS)
    qs = pl.BlockSpec((None, tile, DIL_DH), lambda b, h, n: (b, n, h))
    ks = pl.BlockSpec((None, s, DIL_DH), lambda b, h, n: (b, 0, h))
    return pl.pallas_call(
        functools.partial(_dil_body, tile=tile),
        out_shape=jax.ShapeDtypeStruct((bsz, s, DIL_W), F32),
        grid=(bsz, H_DIL, s // tile),
        in_specs=[qs, ks, ks],
        out_specs=qs,
        scratch_shapes=[pltpu.VMEM((tile, DIL_DH), F32), pltpu.VMEM((tile, LANES), F32),
                        pltpu.VMEM((tile, LANES), F32)],
        compiler_params=_params(("arbitrary", "arbitrary", "arbitrary")),
        name="dil_prompt",
    )(q, k, v)


DIL_GROUP = 16
DIL_TAIL = 512


def _dil_sample_counts(buf, t_new):
    ga = (buf - DIL_TAIL) // DIL_GROUP
    pos_all = np.arange(buf)
    sel = (pos_all >= ga * DIL_GROUP) | (pos_all % DIL_GROUP < t_new)
    t = np.arange(t_new)

    def count(pos, real):
        cnt = np.zeros((t_new, pos.size), np.float32)
        for window, dil in DIL_PAIRS:
            delta = buf + t[:, None] - pos[None, :]
            cnt += (delta >= 0) & (delta % dil == 0) & (delta // dil <= window // dil) & real[None, :]
        return cnt

    assert not count(pos_all[~sel], np.ones((~sel).sum(), bool)).any()
    pad_new = LANES // H_DIL
    pos = np.concatenate([pos_all[sel], buf + np.arange(pad_new)])
    real = np.concatenate([np.ones(sel.sum(), bool), np.arange(pad_new) < t_new])
    cnt = count(pos, real)
    same_head = np.eye(H_DIL, dtype=np.float32)
    full = cnt[:, None, :, None] * same_head[None, :, None, :]
    return full.reshape(t_new * H_DIL, pos.size * H_DIL)


def _dil_sample_body(q_ref, kn_ref, vn_ref, ck_ref, cv_ref, cnt_ref, nk_ref, nv_ref, o_ref, *, ga):
    g = ck_ref.shape[0]
    half = ck_ref.shape[1] // 2
    k_new = kn_ref[...]
    v_new = vn_ref[...]
    for c_ref, n_ref, new in ((ck_ref, nk_ref, k_new), (cv_ref, nv_ref, v_new)):
        n_ref[:, 0:half, :] = c_ref[:, half:, :]
        n_ref[0:g - 1, half:, :] = c_ref[1:g, 0:half, :]
        n_ref[g - 1, half:, :] = new

    zpad = jnp.zeros((LANES - half, DIL_DH), F32)

    def keys(c_ref, new):
        main = c_ref[0:ga, 0:half, :].reshape(ga * half, DIL_DH)
        tail = c_ref[ga:g, :, :].reshape((g - ga) * 2 * half, DIL_DH)
        return jnp.concatenate([main, tail, new, zpad], axis=0).astype(BF16)

    s = _mm_nt(q_ref[...].astype(BF16), keys(ck_ref, k_new)) * (DIL_DH ** -0.5)
    cnt = cnt_ref[...]
    s = jnp.where(cnt > 0.0, s, NEG_INF)
    m = jnp.max(s, axis=-1, keepdims=True)
    p = jnp.exp(s - m) * cnt
    den = jnp.sum(p, axis=-1, keepdims=True)
    o_ref[...] = _mm(p.astype(BF16), keys(cv_ref, v_new)) / den


def _dil_sample(q, k_new, v_new, cache_k, cache_v):
    db, rows_new, _ = q.shape
    g, grows = cache_k.shape[1], cache_k.shape[2]
    assert grows == DIL_GROUP * H_DIL and rows_new * 2 == grows
    buf = g * DIL_GROUP
    ga = (buf - DIL_TAIL) // DIL_GROUP
    cnt = jnp.asarray(_dil_sample_counts(buf, rows_new // H_DIL))
    new = pl.BlockSpec((None, rows_new, DIL_DH), lambda b: (b, 0, 0))
    big = pl.BlockSpec((None, g, grows, DIL_DH), lambda b: (b, 0, 0, 0))
    return pl.pallas_call(
        functools.partial(_dil_sample_body, ga=ga),
        out_shape=[jax.ShapeDtypeStruct(cache_k.shape, F32),
                   jax.ShapeDtypeStruct(cache_v.shape, F32),
                   jax.ShapeDtypeStruct((db, rows_new, DIL_DH), F32)],
        grid=(db,),
        in_specs=[new, new, new, big, big, pl.BlockSpec(cnt.shape, lambda b: (0, 0))],
        out_specs=[big, big, new],
        compiler_params=_params(("arbitrary",)),
        name="dil_sample",
    )(q, k_new, v_new, cache_k, cache_v, cnt)


def _post_a_body(x_ref, og_ref, r_ref, od_ref, gg_ref, wo_ref, lg_ref, lb_ref, wq_ref, x1_o, qm_o):
    x1 = _mixer_out(x_ref[...], og_ref[...], r_ref[...], od_ref[...], gg_ref[...], wo_ref[...],
                    lg_ref[...], lb_ref[...])
    x1_o[...] = x1
    qm_o[...] = _mm(x1.astype(BF16), wq_ref[...]).astype(BF16)


def _post_a(x, og, r, od, gg, wo, lg, lb, wq):
    t = x.shape[0]
    tm = min(TOKEN_TILE, t)
    row = lambda i: (i, 0)
    const = lambda i: (0, 0)
    full = lambda a: pl.BlockSpec(a.shape, const)
    return pl.pallas_call(
        _post_a_body,
        out_shape=[jax.ShapeDtypeStruct((t, D_MODEL), F32), jax.ShapeDtypeStruct((t, D_MODEL), BF16)],
        grid=(t // tm,),
        in_specs=[pl.BlockSpec((tm, D_MODEL), row), pl.BlockSpec((tm, GLA_VAL_W), row),
                  pl.BlockSpec((tm, GLA_VAL_W), row), pl.BlockSpec((tm, DIL_W), row),
                  full(gg), full(wo), full(lg), full(lb), full(wq)],
        out_specs=[pl.BlockSpec((tm, D_MODEL), row), pl.BlockSpec((tm, D_MODEL), row)],
        compiler_params=_params(("arbitrary",)),
        name="post_a",
    )(x, og, r, od, gg, wo, lg, lb, wq)


def _mem_kv_body(m_ref, wk_ref, wv_ref, k_o, v_o):
    mb = m_ref[...].astype(BF16)
    k_o[...] = _mm(mb, wk_ref[...])
    v_o[...] = _mm(mb, wv_ref[...])


def _mem_kv(mem, wk, wv):
    bsz, mt, _ = mem.shape
    blk = pl.BlockSpec((None, mt, D_MODEL), lambda b: (b, 0, 0))
    w = pl.BlockSpec((D_MODEL, D_MODEL), lambda b: (0, 0))
    return pl.pallas_call(
        _mem_kv_body,
        out_shape=[jax.ShapeDtypeStruct((bsz, mt, D_MODEL), F32)] * 2,
        grid=(bsz,),
        in_specs=[blk, w, w],
        out_specs=[blk, blk],
        compiler_params=_params(("arbitrary",)),
        name="mem_kv",
    )(mem, wk, wv)


MEM_HALVES = MEM_DH // LANES


def _mem_attn_rows_body(q_ref, k_ref, v_ref, mask_ref, o_ref, *, g):
    scale = MEM_DH ** -0.5
    hh = MEM_HEADS
    nq = q_ref.shape[1] // MEM_HALVES
    nk = k_ref.shape[1]
    valid = mask_ref[...] > 0.0
    for ig in range(g):
        part = _mm_nt(q_ref[ig], k_ref[ig].astype(BF16))
        s = part[0:nq] + pltpu.roll(part[nq:2 * nq], nk - hh, 1)
        s = jnp.where(valid, s * scale, NEG_INF)
        m = jnp.max(s, axis=-1, keepdims=True)
        p = jnp.exp(s - m)
        p = (p / jnp.sum(p, axis=-1, keepdims=True)).astype(F32)
        both = jnp.concatenate([p, pltpu.roll(p, hh, 1)], axis=0).astype(BF16)
        o_ref[ig] = _mm(both, v_ref[ig].astype(BF16)).astype(BF16)


def _mem_attn_rows(q, mk, mv, *, g):
    nb, qr, _ = q.shape
    kr = mk.shape[1]
    nq = qr // MEM_HALVES
    col = np.arange(kr)
    mask = ((col[None, :] % MEM_HEADS == np.arange(nq)[:, None] % MEM_HEADS)
            & (col[None, :] % (MEM_HEADS * MEM_HALVES) < MEM_HEADS)).astype(np.float32)
    qs = pl.BlockSpec((g, qr, LANES), lambda i: (i, 0, 0))
    ks = pl.BlockSpec((g, kr, LANES), lambda i: (i, 0, 0))
    return pl.pallas_call(
        functools.partial(_mem_attn_rows_body, g=g),
        out_shape=jax.ShapeDtypeStruct((nb, qr, LANES), BF16),
        grid=(nb // g,),
        in_specs=[qs, ks, ks, pl.BlockSpec(mask.shape, lambda i: (0, 0))],
        out_specs=qs,
        compiler_params=_params(("arbitrary",)),
        name="mem_attn_rows",
    )(q, mk, mv, jnp.asarray(mask))


ROUTE_GROUP_LANE0 = N_EXPERTS


def _mixer_out(x, og, r, od, gg, wo, lg, lb):
    parts = []
    for h in range(H_GLA):
        oh = og[:, h * GLA_DV:(h + 1) * GLA_DV]
        parts.append(oh * lax.rsqrt(jnp.mean(oh * oh, axis=-1, keepdims=True) + LN_EPS))
    on = jnp.concatenate(parts, axis=1) * gg * (r * jax.nn.sigmoid(r))
    cat = jnp.concatenate([on.astype(BF16), od.astype(BF16)], axis=1)
    return _layer_norm(ALPHA * x + _mm(cat, wo), lg, lb)


def _route(x2, wr, br, eid_o, ew_o):
    logits = _mm(x2.astype(BF16), wr) + br
    lane = lax.broadcasted_iota(jnp.int32, logits.shape, 1).astype(F32)
    big = float(LANES)

    def first_argmax(vals, vmax):
        return jnp.min(jnp.where(vals == vmax, lane, big), axis=-1, keepdims=True)

    is_g = (lane >= ROUTE_GROUP_LANE0) & (lane < ROUTE_GROUP_LANE0 + N_GROUPS)
    gl = jnp.where(is_g, logits, NEG_INF)
    gmax = jnp.max(gl, axis=-1, keepdims=True)
    g_sel = first_argmax(gl, gmax) - ROUTE_GROUP_LANE0
    g_w = 1.0 / jnp.sum(jnp.where(is_g, jnp.exp(gl - gmax), 0.0), axis=-1, keepdims=True)
    lo = g_sel * EXPERTS_PER_GROUP
    el = jnp.where((lane >= lo) & (lane < lo + EXPERTS_PER_GROUP), logits, NEG_INF)
    v1 = jnp.max(el, axis=-1, keepdims=True)
    i1 = first_argmax(el, v1)
    el2 = jnp.where(lane == i1, NEG_INF, el)
    v2 = jnp.max(el2, axis=-1, keepdims=True)
    i2 = first_argmax(el2, v2)
    e = jnp.exp(v2 - v1)
    w1 = g_w / (1.0 + e)
    w2 = g_w * e / (1.0 + e)
    eid_o[...] = jnp.where(lane == 0.0, i1, jnp.where(lane == 1.0, i2, 0.0)).astype(jnp.int32)
    ew_o[...] = jnp.where(lane == 0.0, w1, jnp.where(lane == 1.0, w2, 0.0))


def _post_body(x_ref, og_ref, r_ref, od_ref, mk_ref, mv_ref, x1s_ref, aos_ref,
               gg_ref, wo_ref, lg1_ref, lb1_ref, wq_ref, wmo_ref, lg2_ref, lb2_ref, wr_ref, br_ref,
               x2_o, x2p_o, eid_o, ew_o, *, prompt_steps):
    def tail(x1, ao):
        x2 = _layer_norm(ALPHA * x1 + _mm(ao, wmo_ref[...]), lg2_ref[...], lb2_ref[...])
        x2_o[...] = x2
        _pack_pairs(x2p_o, x2)
        _route(x2, wr_ref[...], br_ref[...], eid_o, ew_o)

    i = pl.program_id(0)

    @pl.when(i < prompt_steps)
    def _():
        x1 = _mixer_out(x_ref[...], og_ref[...], r_ref[...], od_ref[...], gg_ref[...], wo_ref[...],
                        lg1_ref[...], lb1_ref[...])
        q = _mm(x1.astype(BF16), wq_ref[...]).astype(BF16)
        scale = MEM_DH ** -0.5
        outs = []
        for h in range(MEM_HEADS):
            hs = slice(h * MEM_DH, (h + 1) * MEM_DH)
            s = _mm_nt(q[:, hs], mk_ref[:, hs].astype(BF16)) * scale
            m = jnp.max(s, axis=-1, keepdims=True)
            p = jnp.exp(s - m)
            p = p / jnp.sum(p, axis=-1, keepdims=True)
            outs.append(_mm(p.astype(BF16), mv_ref[:, hs].astype(BF16)))
        tail(x1, jnp.concatenate(outs, axis=1).astype(BF16))

    @pl.when(i >= prompt_steps)
    def _():
        tail(x1s_ref[...], aos_ref[...])


def _post(x_p, og_p, r_p, od_p, mem_k, mem_v, x1_s, ao_s, gg, wo, lg1, lb1, wq, wmo, lg2, lb2, wr, br):
    n_p, n_s = x_p.shape[0], x1_s.shape[0]
    bsz, mt, _ = mem_k.shape
    tm = int(np.gcd(np.gcd(n_p // bsz, n_s), TOKEN_TILE))
    sp, ss = n_p // tm, n_s // tm
    steps_per_b = sp // bsz
    prow = lambda i: (jnp.minimum(i, sp - 1), 0)
    mrow = lambda i: (jnp.minimum(i, sp - 1) // steps_per_b, 0, 0)
    srow = lambda i: (jnp.maximum(i - sp, 0), 0)
    orow = lambda i: (i, 0)
    const = lambda i: (0, 0)
    full = lambda a: pl.BlockSpec(a.shape, const)
    n_total = n_p + n_s
    weights = (gg, wo, lg1, lb1, wq, wmo, lg2, lb2, wr, br)
    return pl.pallas_call(
        functools.partial(_post_body, prompt_steps=sp),
        out_shape=[jax.ShapeDtypeStruct((n_total, D_MODEL), F32),
                   jax.ShapeDtypeStruct((n_total * PAIR_TILES, LANES), jnp.uint32),
                   jax.ShapeDtypeStruct((n_total, LANES), jnp.int32),
                   jax.ShapeDtypeStruct((n_total, LANES), F32)],
        grid=(sp + ss,),
        in_specs=[pl.BlockSpec((tm, D_MODEL), prow), pl.BlockSpec((tm, GLA_VAL_W), prow),
                  pl.BlockSpec((tm, GLA_VAL_W), prow), pl.BlockSpec((tm, DIL_W), prow),
                  pl.BlockSpec((None, mt, D_MODEL), mrow), pl.BlockSpec((None, mt, D_MODEL), mrow),
                  pl.BlockSpec((tm, D_MODEL), srow), pl.BlockSpec((tm, D_MODEL), srow)]
                 + [full(w) for w in weights],
        out_specs=[pl.BlockSpec((tm, D_MODEL), orow), pl.BlockSpec((tm * PAIR_TILES, LANES), orow),
                   pl.BlockSpec((tm, LANES), orow), pl.BlockSpec((tm, LANES), orow)],
        compiler_params=_params(("arbitrary",)),
        name="post",
    )(x_p, og_p, r_p, od_p, mem_k, mem_v, x1_s, ao_s, *weights)


MOE_DMA_UNROLL = 16
MOE_BURSTS = 8
MXU_COLS = 256
MOE_VMEM_LIMIT = 60 * 1024 * 1024


def _moe_body(te_ref, nv_ref, nlive_ref, src0_ref, srcn_ref, dst0_ref, dstn_ref, spare_ref, xp_hbm,
              wg_ref, wu_ref, wd_ref, eo_hbm,
              xp, gbuf, obuf, src_smem, dst_smem, xsem, ssem, isem, dsem, wgb, wub, wdb, *, nt, spare0):
    t = pl.program_id(0)
    tile = MOE_TILE
    slot = t % 2
    rt = ROW_TILES
    pt = PAIR_TILES
    burst = tile // MOE_BURSTS

    def src_copy(idx_vmem, s):
        return pltpu.make_async_copy(idx_vmem.at[0], src_smem.at[s], isem.at[s])

    def dst_copy(idx_vmem, s):
        return pltpu.make_async_copy(idx_vmem.at[0], dst_smem.at[s], dsem.at[s % 2])

    def scatter_wait(s):
        pltpu.make_async_copy(obuf.at[s], eo_hbm.at[pl.ds(0, tile * rt)], ssem.at[s]).wait()

    def scatter_rows(s, d, lo, hi):
        def chunk(c, carry):
            for u in range(MOE_DMA_UNROLL):
                i = c * MOE_DMA_UNROLL + u
                row = pl.multiple_of(dst_smem[d, 0, i], rt)
                pltpu.make_async_copy(obuf.at[s, pl.ds(pl.multiple_of(i * rt, rt), rt)],
                                      eo_hbm.at[pl.ds(row, rt)], ssem.at[s]).start(priority=u % 2)
            return carry
        lax.fori_loop(lo // MOE_DMA_UNROLL, hi // MOE_DMA_UNROLL, chunk, 0)

    def live(i):
        return (i >= 0) & (i < nt) & (nv_ref[jnp.clip(i, 0, nt - 1)] > 0)

    @pl.when(t == 0)
    def _():
        xcp = pltpu.make_async_copy(xp_hbm, xp, xsem)
        xcp.start()
        obuf[...] = jnp.zeros_like(obuf)
        for s in range(2):
            cp = pltpu.make_async_copy(obuf.at[s], eo_hbm.at[pl.ds((spare0 + s * tile) * rt, tile * rt)], ssem.at[s])
            cp.start()
            cp.wait()
        for cp in (src_copy(src0_ref, 0), dst_copy(dst0_ref, 0), dst_copy(spare_ref, 2)):
            cp.start()
            cp.wait()
        xcp.wait()

    @pl.when(live(t + 1))
    def _():
        src_copy(srcn_ref, 1 - slot).start()
        dst_copy(dstn_ref, (t + 1) % 3).start()

    @pl.when((t == 0) | (te_ref[t] != te_ref[jnp.maximum(t - 1, 0)]))
    def _():
        wgb[...] = wg_ref[...].astype(BF16)
        wub[...] = wu_ref[...].astype(BF16)
        wdb[...] = wd_ref[...].astype(BF16)

    prev_dst = (t + 2) % 3

    for s in range(2):
        @pl.when(live(t) & (slot == s))
        def _(s=s):
            def gather(i, carry):
                row = pl.multiple_of(src_smem[s, 0, i], pt)
                gbuf[pl.ds(pl.multiple_of(i * pt, pt), pt), :] = xp[pl.ds(row, pt), :]
                return carry
            lax.fori_loop(0, tile, gather, 0, unroll=16)

            bursts = iter(range(MOE_BURSTS))

            def burst_prev():
                b = next(bursts)
                for i in range(b * burst, (b + 1) * burst):
                    row = pl.multiple_of(dst_smem[prev_dst, 0, i], rt)
                    pltpu.make_async_copy(obuf.at[1 - s, pl.ds(i * rt, rt)], eo_hbm.at[pl.ds(row, rt)],
                                          ssem.at[1 - s]).start(priority=i % 2)

            xb = _unpack_pairs(gbuf, tile)
            nc = MXU_COLS
            hs = []
            for c in range(EXPERT_HIDDEN // nc):
                cols = slice(c * nc, (c + 1) * nc)
                hg = _mm(xb, wgb[:, cols])
                burst_prev()
                hu = _mm(xb, wub[:, cols])
                burst_prev()
                hs.append((hg * jax.nn.sigmoid(hg) * hu).astype(BF16))
            h = jnp.concatenate(hs, axis=1)

            @pl.when(t >= 1)
            def _():
                scatter_wait(s)

            for c in range(D_MODEL // nc):
                out = _mm(h, wdb[:, c * nc:(c + 1) * nc])
                _to_token_tiles(obuf.at[s], out, col0=c * nc)
                burst_prev()

        @pl.when(live(t - 1) & jnp.logical_not(live(t)) & (slot == s))
        def _(s=s):
            scatter_rows(1 - s, prev_dst, 0, tile)

    @pl.when(live(t + 1))
    def _():
        src_copy(srcn_ref, 1 - slot).wait()
        dst_copy(dstn_ref, (t + 1) % 3).wait()

    @pl.when(t == nt - 1)
    def _():
        n_live = nlive_ref[0]

        @pl.when(n_live >= 1)
        def _():
            scatter_wait((n_live - 1) % 2)

        @pl.when(n_live >= 2)
        def _():
            scatter_wait(n_live % 2)


def _moe(tile_expert, n_valid, src, dst, x2p, wg, wu, wd, n_tok):
    nt = tile_expert.shape[0]
    tile = MOE_TILE
    rt = ROW_TILES
    n_live = jnp.sum((n_valid > 0).astype(jnp.int32)).reshape(1)
    spare0 = 2 * n_tok
    spare1 = ((spare0 + tile + jnp.arange(tile, dtype=jnp.int32)) * rt).reshape(1, 1, tile)
    wspec = lambda shape: pl.BlockSpec((None,) + shape, lambda t, te, nv, nl: (te[t], 0, 0))
    ispec = lambda f: pl.BlockSpec((1, 1, tile), lambda t, te, nv, nl: (f(t), 0, 0))
    first = lambda t: 0
    nxt = lambda t: jnp.minimum(t + 1, nt - 1)
    grid_spec = pltpu.PrefetchScalarGridSpec(
        num_scalar_prefetch=3,
        grid=(nt,),
        in_specs=[ispec(first), ispec(nxt), ispec(first), ispec(nxt), ispec(first),
                  pl.BlockSpec(memory_space=pl.ANY),
                  wspec((D_MODEL, EXPERT_HIDDEN)), wspec((D_MODEL, EXPERT_HIDDEN)),
                  wspec((EXPERT_HIDDEN, D_MODEL))],
        out_specs=pl.BlockSpec(memory_space=pl.ANY),
        scratch_shapes=[pltpu.VMEM(x2p.shape, jnp.uint32),
                        pltpu.VMEM((tile * PAIR_TILES, LANES), jnp.uint32),
                        pltpu.VMEM((2, tile * rt, LANES), F32),
                        pltpu.SMEM((2, 1, tile), jnp.int32),
                        pltpu.SMEM((3, 1, tile), jnp.int32),
                        pltpu.SemaphoreType.DMA,
                        pltpu.SemaphoreType.DMA((2,)),
                        pltpu.SemaphoreType.DMA((2,)),
                        pltpu.SemaphoreType.DMA((2,)),
                        pltpu.VMEM((D_MODEL, EXPERT_HIDDEN), BF16),
                        pltpu.VMEM((D_MODEL, EXPERT_HIDDEN), BF16),
                        pltpu.VMEM((EXPERT_HIDDEN, D_MODEL), BF16)])
    return pl.pallas_call(
        functools.partial(_moe_body, nt=nt, spare0=spare0),
        out_shape=jax.ShapeDtypeStruct(((spare0 + 2 * tile) * rt, LANES), F32),
        grid_spec=grid_spec,
        compiler_params=pltpu.CompilerParams(dimension_semantics=("arbitrary",), vmem_limit_bytes=MOE_VMEM_LIMIT),
        name="moe",
    )(tile_expert, n_valid, n_live, src, src, dst, dst, spare1, x2p, wg, wu, wd)


def _final_body(x_ref, ew_ref, e1_ref, e2_ref, lg_ref, lb_ref, y_ref):
    ew = ew_ref[...]
    n = x_ref.shape[0]
    moe = ew[:, 0:1] * _from_token_tiles(e1_ref, n) + ew[:, 1:2] * _from_token_tiles(e2_ref, n)
    y_ref[...] = _layer_norm(ALPHA * x_ref[...] + moe, lg_ref[...], lb_ref[...])


def _final(x2, ew, eo, lg, lb, row0, t):
    n_tok = x2.shape[0]
    tile = int(np.gcd(np.gcd(row0, t), np.gcd(n_tok, TOKEN_TILE)))
    b0 = row0 // tile
    b1 = n_tok // tile
    const = lambda i: (0, 0)
    blk = lambda off: pl.BlockSpec((tile, D_MODEL), lambda i: (off + i, 0))
    tblk = lambda off: pl.BlockSpec((tile * ROW_TILES, LANES), lambda i: (off + i, 0))
    return pl.pallas_call(
        _final_body,
        out_shape=jax.ShapeDtypeStruct((t, D_MODEL), F32),
        grid=(t // tile,),
        in_specs=[blk(b0), pl.BlockSpec((tile, LANES), lambda i: (b0 + i, 0)), tblk(b0), tblk(b1 + b0),
                  pl.BlockSpec(lg.shape, const), pl.BlockSpec(lb.shape, const)],
        out_specs=blk(0),
        compiler_params=_params(("arbitrary",)),
        name="final",
    )(x2, ew, eo, eo, lg, lb)


def _routing_tables(eid, n_tok):
    tile = MOE_TILE
    n_assign = 2 * n_tok
    assert n_assign % tile == 0
    nt = n_assign // tile + N_EXPERTS + 1
    flat = eid.reshape(-1).astype(jnp.int32)
    experts = jnp.arange(N_EXPERTS, dtype=jnp.int32)
    counts = jnp.sum((flat[:, None] == experts[None, :]).astype(jnp.int32), axis=0)
    pad = (-counts) % tile
    unused = 2 * N_EXPERTS
    pad_keys = jnp.where(jnp.arange(tile, dtype=jnp.int32)[None, :] < pad[:, None],
                         2 * experts[:, None] + 1, unused).reshape(-1)
    keys = jnp.concatenate([2 * flat, pad_keys, jnp.full((tile,), unused, jnp.int32)])
    vals = jnp.concatenate([jnp.arange(n_assign, dtype=jnp.int32),
                            jnp.full(((N_EXPERTS + 1) * tile,), -1, jnp.int32)])
    keys, vals = lax.sort((keys, vals), num_keys=1)
    keys = keys.reshape(nt, tile)
    vals = vals.reshape(nt, tile)
    tile_expert = jnp.minimum(keys[:, 0] // 2, N_EXPERTS - 1)
    real = vals >= 0
    n_valid = jnp.sum(real.astype(jnp.int32), axis=1)
    a = jnp.maximum(vals, 0)
    src = a >> 1
    spare = 2 * n_tok + (jnp.arange(nt, dtype=jnp.int32)[:, None] % 2) * tile + jnp.arange(tile, dtype=jnp.int32)[None, :]
    dst = jnp.where(real, (a & 1) * n_tok + (a >> 1), spare)
    return tile_expert, n_valid, src.reshape(nt, 1, tile), dst.reshape(nt, 1, tile)


def kernel(x_prompt, x_sample, mem_prompt, cache_dil_k, cache_dil_v, state_gla, cache_mem_k, cache_mem_v, w_in, w_gate_lr, b_gate, g_gla_norm, w_out, ln_mix_g, ln_mix_b, w_mem_q, w_mem_k, w_mem_v, w_mem_o, ln_mem_g, ln_mem_b, w_route_group, b_route_group, w_route_expert, b_route_expert, w_exp_gate, w_exp_up, w_exp_down, ln_ffn_g, ln_ffn_b):
    assert w_in.shape[0] == DEPTH
    bp, seq, _ = x_prompt.shape
    db, tdec, _ = x_sample.shape
    buf = cache_dil_k.shape[2]
    mt = mem_prompt.shape[1]
    n_p, n_s = bp * seq, db * tdec
    n_tok = n_p + n_s
    l = 0

    a0 = 2 * GLA_KEY_W + 2 * GLA_VAL_W
    w_main = jnp.concatenate([w_in[l][:, :a0], w_in[l][:, a0 + GATE_RANK:]], axis=1).astype(BF16)
    w_a = jnp.pad(w_in[l][:, a0:a0 + GATE_RANK], ((0, 0), (0, LANES - GATE_RANK))).astype(BF16)
    w_gl = jnp.pad(w_gate_lr[l], ((0, LANES - GATE_RANK), (0, 0))).astype(BF16)
    b_g = b_gate[l][None, :]
    row1 = lambda a: a[l][None, :]
    w_o = w_out[l].astype(BF16)
    w_q = w_mem_q[l].astype(BF16)
    w_k = w_mem_k[l].astype(BF16)
    w_v = w_mem_v[l].astype(BF16)
    w_mo = w_mem_o[l].astype(BF16)
    w_r = jnp.pad(jnp.concatenate([w_route_expert[l], w_route_group[l]], axis=1),
                  ((0, 0), (0, LANES - N_EXPERTS - N_GROUPS))).astype(BF16)
    b_r = jnp.pad(jnp.concatenate([b_route_expert[l], b_route_group[l]]), (0, LANES - N_EXPERTS - N_GROUPS))[None, :]

    def mixer_inputs(x2d):
        return _proj(x2d, w_main, w_a, w_gl, b_g)

    def post_a(x2d, o_g, r_g, o_d):
        return _post_a(x2d, o_g, r_g, o_d, row1(g_gla_norm), w_o, row1(ln_mix_g), row1(ln_mix_b), w_q)

    xp = x_prompt.reshape(n_p, D_MODEL)
    qg, kg, vg, rg, la, qd, kd, vd = mixer_inputs(xp)
    sh = lambda t: t.reshape(bp, seq, t.shape[-1])
    s0 = jnp.zeros((bp, H_GLA, GLA_DK, GLA_DV), F32)
    o_g, s_fin_p = _gla(sh(qg), sh(kg), sh(vg), sh(la), s0, bb=1, nchunk=4)
    o_d = _dil_prompt(sh(qd), sh(kd), sh(vd))
    mem_k, mem_v = _mem_kv(mem_prompt, w_k, w_v)
    prompt_mix = (xp, o_g.reshape(n_p, -1), rg, o_d.reshape(n_p, -1))
    keep = min(WINDOW_MAX, seq)
    dk_p = sh(kd)[:, seq - keep:].reshape(1, bp, keep, H_DIL, DIL_DH)
    dv_p = sh(vd)[:, seq - keep:].reshape(1, bp, keep, H_DIL, DIL_DH)

    xs = x_sample.reshape(n_s, D_MODEL)
    qg, kg, vg, rg, la, qd, kd, vd = mixer_inputs(xs)
    shs = lambda t: t.reshape(db, tdec, t.shape[-1])
    gb = 8 if db % 8 == 0 else 1
    o_g, s_new = _gla(shs(qg), shs(kg), shs(vg), shs(la), state_gla[l], bb=gb, nchunk=1)
    head_rows = lambda t: t.reshape(db, tdec * H_DIL, DIL_DH)
    groups = lambda c: c[l].reshape(db, buf // DIL_GROUP, DIL_GROUP * H_DIL, DIL_DH)
    nk, nv, o_d = _dil_sample(head_rows(qd), head_rows(kd), head_rows(vd),
                              groups(cache_dil_k), groups(cache_dil_v))
    x1_s, qm = post_a(xs, o_g.reshape(n_s, -1), rg, o_d.reshape(n_s, DIL_W))
    gm = 8 if db % 8 == 0 else 1

    def cache_rows(c):
        return (c[l].reshape(db, mt, MEM_HEADS, MEM_HALVES, LANES).transpose(0, 1, 3, 2, 4)
                .reshape(db, mt * MEM_HALVES * MEM_HEADS, LANES))

    q_rows = (qm.reshape(db, tdec, MEM_HEADS, MEM_HALVES, LANES).transpose(0, 3, 1, 2, 4)
              .reshape(db, MEM_HALVES * tdec * MEM_HEADS, LANES))
    ao_s = _mem_attn_rows(q_rows, cache_rows(cache_mem_k), cache_rows(cache_mem_v), g=gm)
    ao_s = (ao_s.reshape(db, MEM_HALVES, tdec, MEM_HEADS, LANES).transpose(0, 2, 3, 1, 4).reshape(n_s, D_MODEL))

    x2, x2p, eid, ew = _post(*prompt_mix, mem_k, mem_v, x1_s, ao_s, row1(g_gla_norm), w_o, row1(ln_mix_g),
                             row1(ln_mix_b), w_q, w_mo, row1(ln_mem_g), row1(ln_mem_b), w_r, b_r)
    tile_expert, n_valid, src, dst = _routing_tables(eid[:, :2], n_tok)
    eo = _moe(tile_expert, n_valid, src * PAIR_TILES, dst * ROW_TILES, x2p,
              w_exp_gate[l], w_exp_up[l], w_exp_down[l], n_tok)
    y_p = _final(x2, ew, eo, row1(ln_ffn_g), row1(ln_ffn_b), 0, n_p)
    y_s = _final(x2, ew, eo, row1(ln_ffn_g), row1(ln_ffn_b), n_p, n_s)

    return (y_p.reshape(bp, seq, D_MODEL), y_s.reshape(db, tdec, D_MODEL),
            dk_p, dv_p, s_fin_p[None], mem_k.reshape(1, bp, mt, MEM_HEADS, MEM_DH),
            mem_v.reshape(1, bp, mt, MEM_HEADS, MEM_DH),
            nk.reshape(1, db, buf, H_DIL, DIL_DH), nv.reshape(1, db, buf, H_DIL, DIL_DH), s_new[None])
```

```python
import functools

import numpy as np
import jax
import jax.numpy as jnp
from jax import lax
from jax.experimental import pallas as pl
from jax.experimental.pallas import tpu as pltpu

F32 = jnp.float32
BF16 = jnp.bfloat16

D_MODEL = 1024
H_GLA = 4
GLA_DK = 64
GLA_DV = 128
GLA_KEY_W = H_GLA * GLA_DK
GLA_VAL_W = H_GLA * GLA_DV
GATE_RANK = 16
GATE_TAU = 16.0
GLA_CHUNK = 64
H_DIL = 4
DIL_DH = 128
DIL_W = H_DIL * DIL_DH
DIL_PAIRS = ((128, 1), (512, 4), (2048, 16))
DIL_SPAN = 128
WINDOW_MAX = 2048
MEM_HEADS = 4
MEM_DH = 256
N_GROUPS = 4
EXPERTS_PER_GROUP = 8
N_EXPERTS = N_GROUPS * EXPERTS_PER_GROUP
EXPERT_HIDDEN = 512
DEPTH = 1
ALPHA = (2.0 * DEPTH) ** 0.25
LN_EPS = 1e-5
NEG_INF = -1e30

LANES = 128
VMEM_LIMIT = 56 * 1024 * 1024

TOKEN_TILE = 512
MOE_TILE = 256


def _mm(a, b):
    return jnp.dot(a, b, preferred_element_type=F32)


def _mm_nt(a, b):
    return lax.dot_general(a, b, (((1,), (1,)), ((), ())), preferred_element_type=F32)


def _mm_tn(a, b):
    return lax.dot_general(a, b, (((0,), (0,)), ((), ())), preferred_element_type=F32)


def _layer_norm(v, g, b):
    mu = jnp.mean(v, axis=-1, keepdims=True)
    d = v - mu
    var = jnp.mean(d * d, axis=-1, keepdims=True)
    return d * lax.rsqrt(var + LN_EPS) * g + b


def _params(sem):
    return pltpu.CompilerParams(dimension_semantics=sem, vmem_limit_bytes=VMEM_LIMIT)


ROW_TILES = D_MODEL // LANES


def _to_token_tiles(ref, val, col0=0):
    n = val.shape[0]
    for s in range(val.shape[1] // LANES):
        ref[pl.ds(col0 // LANES + s, n, stride=ROW_TILES), :] = val[:, s * LANES:(s + 1) * LANES]


def _from_token_tiles(ref, n):
    return jnp.concatenate([ref[pl.ds(s, n, stride=ROW_TILES), :] for s in range(ROW_TILES)], axis=1)


PAIR_TILES = D_MODEL // 2 // LANES


def _pack_pairs(ref, val):
    n = val.shape[0]
    half = D_MODEL // 2
    bits = lambda v: pltpu.bitcast(v.astype(BF16).astype(F32), jnp.uint32)
    words = (bits(val[:, :half]) >> 16) | (bits(val[:, half:]) & jnp.uint32(0xFFFF0000))
    for s in range(PAIR_TILES):
        ref[pl.ds(s, n, stride=PAIR_TILES), :] = words[:, s * LANES:(s + 1) * LANES]


def _unpack_pairs(ref, n):
    words = [ref[pl.ds(s, n, stride=PAIR_TILES), :] for s in range(PAIR_TILES)]
    lo = [pltpu.bitcast(w << 16, F32).astype(BF16) for w in words]
    hi = [pltpu.bitcast(w & jnp.uint32(0xFFFF0000), F32).astype(BF16) for w in words]
    return jnp.concatenate(lo + hi, axis=1)


def _to_head_rows(ref, val):
    n = val.shape[0]
    for h in range(H_DIL):
        ref[pl.ds(h, n, stride=H_DIL), :] = val[:, h * DIL_DH:(h + 1) * DIL_DH]


def _proj_body(x_ref, w_ref, wa_ref, wgl_ref, bg_ref,
               qg_o, kg_o, vg_o, r_o, la_o, qd_o, kd_o, vd_o, *row_outs):
    xb = x_ref[...].astype(BF16)

    def mm(lo, hi):
        return _mm(xb, w_ref[:, lo:hi])

    qg_o[...] = mm(0, 256) * (GLA_DK ** -0.5)
    kg_o[...] = mm(256, 512)
    vg_o[...] = mm(512, 1024)
    r_o[...] = mm(1024, 1536)
    qd = mm(1536, 2048)
    kd = mm(2048, 2560)
    vd = mm(2560, 3072)
    qd_o[...] = qd
    kd_o[...] = kd
    vd_o[...] = vd
    for ref, val in zip(row_outs, (kd, vd, qd)):
        _to_head_rows(ref, val)
    a_lr = _mm(xb, wa_ref[...])
    z = _mm(a_lr.astype(BF16), wgl_ref[...]) + bg_ref[...]
    la_o[...] = (jnp.minimum(z, 0.0) - jnp.log1p(jnp.exp(-jnp.abs(z)))) * (1.0 / GATE_TAU)


def _proj(x, w_main, w_a, w_gl, b_g, *, seq, keep, q_rows):
    t = x.shape[0]
    tm = min(TOKEN_TILE, t)
    widths = (256, 256, 512, 512, 256, 512, 512, 512)
    row = lambda i: (i, 0)
    const = lambda i: (0, 0)
    if keep == seq:
        kept = row
    else:
        assert seq % tm == 0 and keep % tm == 0
        per_seq, per_keep = seq // tm, keep // tm
        kept = lambda i: ((i // per_seq) * per_keep + jnp.maximum(i % per_seq - (per_seq - per_keep), 0), 0)
    n_rows = 3 if q_rows else 2
    n_kept = t // seq * keep
    return pl.pallas_call(
        _proj_body,
        out_shape=[jax.ShapeDtypeStruct((t, w), F32) for w in widths]
                  + [jax.ShapeDtypeStruct((n_kept * H_DIL, DIL_DH), F32)] * n_rows,
        grid=(t // tm,),
        in_specs=[pl.BlockSpec((tm, D_MODEL), row),
                  pl.BlockSpec(w_main.shape, const),
                  pl.BlockSpec(w_a.shape, const),
                  pl.BlockSpec(w_gl.shape, const),
                  pl.BlockSpec(b_g.shape, const)],
        out_specs=[pl.BlockSpec((tm, w), row) for w in widths]
                  + [pl.BlockSpec((tm * H_DIL, DIL_DH), kept)] * n_rows,
        compiler_params=_params(("arbitrary",)),
        name="proj",
    )(x, w_main, w_a, w_gl, b_g)


GLA_ROWS = 256


def _split3(a):
    a1 = a.astype(BF16)
    r1 = a - a1.astype(F32)
    a2 = r1.astype(BF16)
    r2 = r1 - a2.astype(F32)
    return a1, a2, r2.astype(BF16)


def _gla_body(q_ref, k_ref, v_ref, la_ref, s0_ref, o_ref, sfin_ref, s_scr, *, rows, blk, chain):
    nb = rows // blk
    j = pl.program_id(1)
    if chain:
        @pl.when(j == 0)
        def _():
            s_scr[...] = s0_ref[0]

    ri = lax.broadcasted_iota(jnp.int32, (rows, rows), 0)
    ci = lax.broadcasted_iota(jnp.int32, (rows, rows), 1)
    causal = (ci <= ri) & (ri // blk == ci // blk)
    tri = jnp.where(causal, 1.0, 0.0).astype(BF16)
    rblk = lax.broadcasted_iota(jnp.int32, (rows, nb * GLA_DV), 0) // blk
    cblk = lax.broadcasted_iota(jnp.int32, (rows, nb * GLA_DV), 1) // GLA_DV
    own = rblk == cblk

    q = q_ref[...].reshape(rows, GLA_KEY_W)
    k = k_ref[...].reshape(rows, GLA_KEY_W)
    v = v_ref[...].reshape(rows, GLA_VAL_W)
    a = la_ref[...].reshape(rows, GLA_KEY_W)

    cum = _mm(tri, jnp.concatenate(_split3(a), axis=1))
    b = cum[:, :GLA_KEY_W] + cum[:, GLA_KEY_W:2 * GLA_KEY_W] + cum[:, 2 * GLA_KEY_W:]
    b_last = jnp.concatenate([b[(i + 1) * blk - 1:(i + 1) * blk] for i in range(nb)], axis=0)
    b_end = jnp.concatenate([jnp.broadcast_to(b_last[i:i + 1], (blk, GLA_KEY_W)) for i in range(nb)], axis=0)
    q_t = (q * jnp.exp(b)).astype(BF16)
    k_t = (k * jnp.exp(-b)).astype(BF16)
    k_end = (k * jnp.exp(b_end - b)).astype(BF16)
    vb = v.astype(BF16)
    dec = jnp.transpose(jnp.concatenate([jnp.exp(b_last), jnp.zeros((LANES - nb, GLA_KEY_W), F32)], axis=0))

    outs = []
    for h in range(H_GLA):
        hk = slice(h * GLA_DK, (h + 1) * GLA_DK)
        vh = vb[:, h * GLA_DV:(h + 1) * GLA_DV]
        a_in = jnp.where(causal, _mm_nt(q_t[:, hk], k_t[:, hk]), 0.0).astype(BF16)
        o_h = _mm(a_in, vh)
        v_exp = jnp.where(own, jnp.concatenate([vh] * nb, axis=1), jnp.zeros((), BF16))
        kv = _mm_tn(k_end[:, hk], v_exp)
        states = []
        for i in range(nb):
            if chain:
                s_i = s_scr[h] if i == 0 else s_next
            else:
                s_i = s0_ref[i, h]
            states.append(s_i.astype(BF16))
            d_i = jnp.broadcast_to(dec[hk, i:i + 1], (GLA_DK, GLA_DV))
            s_next = d_i * s_i + kv[:, i * GLA_DV:(i + 1) * GLA_DV]
            if not chain:
                sfin_ref[i, h] = s_next
        if chain:
            s_scr[h] = s_next
        inter = _mm(q_t[:, hk], jnp.concatenate(states, axis=1))
        inter = jnp.where(own, inter, 0.0)
        for i in range(nb):
            o_h = o_h + inter[:, i * GLA_DV:(i + 1) * GLA_DV]
        outs.append(o_h)
    o_ref[...] = jnp.concatenate(outs, axis=1).reshape(o_ref.shape)

    if chain:
        @pl.when(j == pl.num_programs(1) - 1)
        def _():
            sfin_ref[0] = s_scr[...]


def _gla(q, k, v, la, s0, *, chain):
    nbat, s, _ = q.shape
    if chain:
        bb, rows, blk = 1, min(GLA_ROWS, s), GLA_CHUNK
        assert s % rows == 0 and rows % blk == 0
    else:
        bb, rows, blk = GLA_CHUNK // s, s, s
        assert GLA_CHUNK % s == 0 and nbat % bb == 0
    tok = lambda i, j: (i, j, 0)
    st = lambda i, j: (i, 0, 0, 0)
    body = functools.partial(_gla_body, rows=bb * rows, blk=blk, chain=chain)
    return pl.pallas_call(
        body,
        out_shape=[jax.ShapeDtypeStruct((nbat, s, GLA_VAL_W), F32),
                   jax.ShapeDtypeStruct((nbat, H_GLA, GLA_DK, GLA_DV), F32)],
        grid=(nbat // bb, s // rows),
        in_specs=[pl.BlockSpec((bb, rows, GLA_KEY_W), tok),
                  pl.BlockSpec((bb, rows, GLA_KEY_W), tok),
                  pl.BlockSpec((bb, rows, GLA_VAL_W), tok),
                  pl.BlockSpec((bb, rows, GLA_KEY_W), tok),
                  pl.BlockSpec((bb, H_GLA, GLA_DK, GLA_DV), st)],
        out_specs=[pl.BlockSpec((bb, rows, GLA_VAL_W), tok),
                   pl.BlockSpec((bb, H_GLA, GLA_DK, GLA_DV), st)],
        scratch_shapes=[pltpu.VMEM((H_GLA, GLA_DK, GLA_DV), F32)],
        compiler_params=_params(("arbitrary", "arbitrary")),
        name="gla",
    )(q, k, v, la, s0)


DIL_QBLK = 128
DIL_TILE = 2048
DIL_UNROLL = 16


def _dil_body(q_ref, k_ref, v_ref, o_ref, acc_ref, m_ref, l_ref, *, tile):
    blk = DIL_QBLK
    t0 = pl.program_id(2) * tile
    qscale = (DIL_DH ** -0.5) * float(np.log2(np.e))
    ri = lax.broadcasted_iota(jnp.int32, (blk, 2 * blk), 0)
    ci = lax.broadcasted_iota(jnp.int32, (blk, 2 * blk), 1)
    band_bias = jnp.where((ci >= ri) & (ci <= ri + DIL_SPAN), 0.0, NEG_INF)
    prev_cols = lax.broadcasted_iota(jnp.int32, (1, 2 * blk), 1) < blk

    def rows(start, dil):
        return pl.ds(start, blk) if dil == 1 else pl.ds(start, blk, stride=dil)

    def block(dil, q0, first, last):
        qr = rows(q0, dil)
        cur0 = t0 + q0
        prev0 = cur0 - dil * blk
        prev_bias = jnp.where(prev_cols & (prev0 < 0), NEG_INF, 0.0)
        prev0 = jnp.maximum(prev0, 0)
        kw = jnp.concatenate([k_ref[rows(prev0, dil), :], k_ref[rows(cur0, dil), :]], axis=0)
        vw = jnp.concatenate([v_ref[rows(prev0, dil), :], v_ref[rows(cur0, dil), :]], axis=0)
        s = _mm_nt((q_ref[qr, :] * qscale).astype(BF16), kw.astype(BF16)) + band_bias + prev_bias
        m_blk = jnp.max(s, axis=-1, keepdims=True)
        p = jnp.exp2(s - m_blk)
        l_blk = jnp.sum(p, axis=-1, keepdims=True)
        acc = _mm(p.astype(BF16), vw.astype(BF16))
        m_new = jnp.broadcast_to(m_blk, (blk, LANES))
        l_new = jnp.broadcast_to(l_blk, (blk, LANES))
        if not first:
            m_old = m_ref[qr, :]
            m_new = jnp.maximum(m_old, m_new)
            a_old = jnp.exp2(m_old - m_new)
            a_blk = jnp.exp2(m_blk - m_new)
            l_new = a_old * l_ref[qr, :] + a_blk * l_blk
            acc = a_old * acc_ref[qr, :] + a_blk * acc
        if last:
            o_ref[qr, :] = acc / l_new
        else:
            acc_ref[qr, :] = acc
            m_ref[qr, :] = m_new
            l_ref[qr, :] = l_new

    dils = sorted((d for _, d in DIL_PAIRS), reverse=True)
    for bi, dil in enumerate(dils):
        per_res = tile // (dil * blk)

        def body(i, carry, dil=dil, per_res=per_res, bi=bi):
            q0 = (i % dil) + (i // dil) * (dil * blk) if dil > 1 else pl.multiple_of(i * blk, blk)
            block(dil, q0, bi == 0, bi == len(dils) - 1)
            return carry

        lax.fori_loop(0, dil * per_res, body, 0, unroll=DIL_UNROLL)


def _dil_prompt(q, k, v):
    bsz, s, _ = q.shape
    tile = min(DIL_TILE, s)
    assert s % tile == 0 and all(tile % (d * DIL_QBLK) == 0 for _, d in DIL_PAIRS)
    assert all(w // d == DIL_SPAN for w, d in DIL_PAIRS)
    qs = pl.BlockSpec((None, tile, DIL_DH), lambda b, h, n: (b, n, h))
    ks = pl.BlockSpec((None, s, DIL_DH), lambda b, h, n: (b, 0, h))
    return pl.pallas_call(
        functools.partial(_dil_body, tile=tile),
        out_shape=jax.ShapeDtypeStruct((bsz, s, DIL_W), F32),
        grid=(bsz, H_DIL, s // tile),
        in_specs=[qs, ks, ks],
        out_specs=qs,
        scratch_shapes=[pltpu.VMEM((tile, DIL_DH), F32), pltpu.VMEM((tile, LANES), F32),
                        pltpu.VMEM((tile, LANES), F32)],
        compiler_params=_params(("arbitrary", "arbitrary", "arbitrary")),
        name="dil_prompt",
    )(q, k, v)


DIL_GROUP = 16
DIL_TAIL = 512


def _dil_sample_counts(buf, t_new):
    ga = (buf - DIL_TAIL) // DIL_GROUP
    pos_all = np.arange(buf)
    sel = (pos_all >= ga * DIL_GROUP) | (pos_all % DIL_GROUP < t_new)
    t = np.arange(t_new)

    def count(pos, real):
        cnt = np.zeros((t_new, pos.size), np.float32)
        for window, dil in DIL_PAIRS:
            delta = buf + t[:, None] - pos[None, :]
            cnt += (delta >= 0) & (delta % dil == 0) & (delta // dil <= window // dil) & real[None, :]
        return cnt

    assert not count(pos_all[~sel], np.ones((~sel).sum(), bool)).any()
    pad_new = LANES // H_DIL
    pos = np.concatenate([pos_all[sel], buf + np.arange(pad_new)])
    real = np.concatenate([np.ones(sel.sum(), bool), np.arange(pad_new) < t_new])
    cnt = count(pos, real)
    same_head = np.eye(H_DIL, dtype=np.float32)
    full = cnt[:, None, :, None] * same_head[None, :, None, :]
    return full.reshape(t_new * H_DIL, pos.size * H_DIL)


def _dil_sample_body(q_ref, kn_ref, vn_ref, ck_ref, cv_ref, cnt_ref, nk_ref, nv_ref, o_ref, *, ga):
    g = ck_ref.shape[0]
    half = ck_ref.shape[1] // 2
    k_new = kn_ref[...]
    v_new = vn_ref[...]
    for c_ref, n_ref, new in ((ck_ref, nk_ref, k_new), (cv_ref, nv_ref, v_new)):
        n_ref[:, 0:half, :] = c_ref[:, half:, :]
        n_ref[0:g - 1, half:, :] = c_ref[1:g, 0:half, :]
        n_ref[g - 1, half:, :] = new

    zpad = jnp.zeros((LANES - half, DIL_DH), F32)

    def keys(c_ref, new):
        main = c_ref[0:ga, 0:half, :].reshape(ga * half, DIL_DH)
        tail = c_ref[ga:g, :, :].reshape((g - ga) * 2 * half, DIL_DH)
        return jnp.concatenate([main, tail, new, zpad], axis=0).astype(BF16)

    s = _mm_nt(q_ref[...].astype(BF16), keys(ck_ref, k_new)) * (DIL_DH ** -0.5)
    cnt = cnt_ref[...]
    s = jnp.where(cnt > 0.0, s, NEG_INF)
    m = jnp.max(s, axis=-1, keepdims=True)
    p = jnp.exp(s - m) * cnt
    den = jnp.sum(p, axis=-1, keepdims=True)
    o_ref[...] = _mm(p.astype(BF16), keys(cv_ref, v_new)) / den


def _dil_sample(q, k_new, v_new, cache_k, cache_v):
    db, rows_new, _ = q.shape
    g, grows = cache_k.shape[1], cache_k.shape[2]
    assert grows == DIL_GROUP * H_DIL and rows_new * 2 == grows
    buf = g * DIL_GROUP
    ga = (buf - DIL_TAIL) // DIL_GROUP
    cnt = jnp.asarray(_dil_sample_counts(buf, rows_new // H_DIL))
    new = pl.BlockSpec((None, rows_new, DIL_DH), lambda b: (b, 0, 0))
    big = pl.BlockSpec((None, g, grows, DIL_DH), lambda b: (b, 0, 0, 0))
    return pl.pallas_call(
        functools.partial(_dil_sample_body, ga=ga),
        out_shape=[jax.ShapeDtypeStruct(cache_k.shape, F32),
                   jax.ShapeDtypeStruct(cache_v.shape, F32),
                   jax.ShapeDtypeStruct((db, rows_new, DIL_DH), F32)],
        grid=(db,),
        in_specs=[new, new, new, big, big, pl.BlockSpec(cnt.shape, lambda b: (0, 0))],
        out_specs=[big, big, new],
        compiler_params=_params(("arbitrary",)),
        name="dil_sample",
    )(q, k_new, v_new, cache_k, cache_v, cnt)


def _post_a_body(x_ref, og_ref, r_ref, od_ref, gg_ref, wo_ref, lg_ref, lb_ref, wq_ref, x1_o, qm_o):
    x1 = _mixer_out(x_ref[...], og_ref[...], r_ref[...], od_ref[...], gg_ref[...], wo_ref[...],
                    lg_ref[...], lb_ref[...])
    x1_o[...] = x1
    qm_o[...] = _mm(x1.astype(BF16), wq_ref[...]).astype(BF16)


def _post_a(x, og, r, od, gg, wo, lg, lb, wq):
    t = x.shape[0]
    tm = min(TOKEN_TILE, t)
    row = lambda i: (i, 0)
    const = lambda i: (0, 0)
    full = lambda a: pl.BlockSpec(a.shape, const)
    return pl.pallas_call(
        _post_a_body,
        out_shape=[jax.ShapeDtypeStruct((t, D_MODEL), F32), jax.ShapeDtypeStruct((t, D_MODEL), BF16)],
        grid=(t // tm,),
        in_specs=[pl.BlockSpec((tm, D_MODEL), row), pl.BlockSpec((tm, GLA_VAL_W), row),
                  pl.BlockSpec((tm, GLA_VAL_W), row), pl.BlockSpec((tm, DIL_W), row),
                  full(gg), full(wo), full(lg), full(lb), full(wq)],
        out_specs=[pl.BlockSpec((tm, D_MODEL), row), pl.BlockSpec((tm, D_MODEL), row)],
        compiler_params=_params(("arbitrary",)),
        name="post_a",
    )(x, og, r, od, gg, wo, lg, lb, wq)


def _mem_kv_body(m_ref, wk_ref, wv_ref, k_o, v_o):
    mb = m_ref[...].astype(BF16)
    k_o[...] = _mm(mb, wk_ref[...])
    v_o[...] = _mm(mb, wv_ref[...])


def _mem_kv(mem, wk, wv):
    bsz, mt, _ = mem.shape
    blk = pl.BlockSpec((None, mt, D_MODEL), lambda b: (b, 0, 0))
    w = pl.BlockSpec((D_MODEL, D_MODEL), lambda b: (0, 0))
    return pl.pallas_call(
        _mem_kv_body,
        out_shape=[jax.ShapeDtypeStruct((bsz, mt, D_MODEL), F32)] * 2,
        grid=(bsz,),
        in_specs=[blk, w, w],
        out_specs=[blk, blk],
        compiler_params=_params(("arbitrary",)),
        name="mem_kv",
    )(mem, wk, wv)


MEM_HALVES = MEM_DH // LANES


def _mem_attn_rows_body(q_ref, k_ref, v_ref, mask_ref, o_ref, *, g):
    scale = MEM_DH ** -0.5
    hh = MEM_HEADS
    nq = q_ref.shape[1] // MEM_HALVES
    nk = k_ref.shape[1]
    valid = mask_ref[...] > 0.0
    for ig in range(g):
        part = _mm_nt(q_ref[ig], k_ref[ig].astype(BF16))
        s = part[0:nq] + pltpu.roll(part[nq:2 * nq], nk - hh, 1)
        s = jnp.where(valid, s * scale, NEG_INF)
        m = jnp.max(s, axis=-1, keepdims=True)
        p = jnp.exp(s - m)
        p = (p / jnp.sum(p, axis=-1, keepdims=True)).astype(F32)
        both = jnp.concatenate([p, pltpu.roll(p, hh, 1)], axis=0).astype(BF16)
        o_ref[ig] = _mm(both, v_ref[ig].astype(BF16)).astype(BF16)


def _mem_attn_rows(q, mk, mv, *, g):
    nb, qr, _ = q.shape
    kr = mk.shape[1]
    nq = qr // MEM_HALVES
    col = np.arange(kr)
    mask = ((col[None, :] % MEM_HEADS == np.arange(nq)[:, None] % MEM_HEADS)
            & (col[None, :] % (MEM_HEADS * MEM_HALVES) < MEM_HEADS)).astype(np.float32)
    qs = pl.BlockSpec((g, qr, LANES), lambda i: (i, 0, 0))
    ks = pl.BlockSpec((g, kr, LANES), lambda i: (i, 0, 0))
    return pl.pallas_call(
        functools.partial(_mem_attn_rows_body, g=g),
        out_shape=jax.ShapeDtypeStruct((nb, qr, LANES), BF16),
        grid=(nb // g,),
        in_specs=[qs, ks, ks, pl.BlockSpec(mask.shape, lambda i: (0, 0))],
        out_specs=qs,
        compiler_params=_params(("arbitrary",)),
        name="mem_attn_rows",
    )(q, mk, mv, jnp.asarray(mask))


ROUTE_GROUP_LANE0 = N_EXPERTS


def _mixer_out(x, og, r, od, gg, wo, lg, lb):
    parts = []
    for h in range(H_GLA):
        oh = og[:, h * GLA_DV:(h + 1) * GLA_DV]
        parts.append(oh * lax.rsqrt(jnp.mean(oh * oh, axis=-1, keepdims=True) + LN_EPS))
    on = jnp.concatenate(parts, axis=1) * gg * (r * jax.nn.sigmoid(r))
    cat = jnp.concatenate([on.astype(BF16), od.astype(BF16)], axis=1)
    return _layer_norm(ALPHA * x + _mm(cat, wo), lg, lb)


def _route(x2, wr, br, eid_o, ew_o):
    logits = _mm(x2.astype(BF16), wr) + br
    lane = lax.broadcasted_iota(jnp.int32, logits.shape, 1).astype(F32)
    big = float(LANES)

    def first_argmax(vals, vmax):
        return jnp.min(jnp.where(vals == vmax, lane, big), axis=-1, keepdims=True)

    is_g = (lane >= ROUTE_GROUP_LANE0) & (lane < ROUTE_GROUP_LANE0 + N_GROUPS)
    gl = jnp.where(is_g, logits, NEG_INF)
    gmax = jnp.max(gl, axis=-1, keepdims=True)
    g_sel = first_argmax(gl, gmax) - ROUTE_GROUP_LANE0
    g_w = 1.0 / jnp.sum(jnp.where(is_g, jnp.exp(gl - gmax), 0.0), axis=-1, keepdims=True)
    lo = g_sel * EXPERTS_PER_GROUP
    el = jnp.where((lane >= lo) & (lane < lo + EXPERTS_PER_GROUP), logits, NEG_INF)
    v1 = jnp.max(el, axis=-1, keepdims=True)
    i1 = first_argmax(el, v1)
    el2 = jnp.where(lane == i1, NEG_INF, el)
    v2 = jnp.max(el2, axis=-1, keepdims=True)
    i2 = first_argmax(el2, v2)
    e = jnp.exp(v2 - v1)
    w1 = g_w / (1.0 + e)
    w2 = g_w * e / (1.0 + e)
    eid_o[...] = jnp.where(lane == 0.0, i1, jnp.where(lane == 1.0, i2, 0.0)).astype(jnp.int32)
    ew_o[...] = jnp.where(lane == 0.0, w1, jnp.where(lane == 1.0, w2, 0.0))


def _post_body(x_ref, og_ref, r_ref, od_ref, mk_ref, mv_ref, x1s_ref, aos_ref,
               gg_ref, wo_ref, lg1_ref, lb1_ref, wq_ref, wmo_ref, lg2_ref, lb2_ref, wr_ref, br_ref,
               x2_o, x2p_o, eid_o, ew_o, *, prompt_steps):
    def tail(x1, ao):
        x2 = _layer_norm(ALPHA * x1 + _mm(ao, wmo_ref[...]), lg2_ref[...], lb2_ref[...])
        x2_o[...] = x2
        _pack_pairs(x2p_o, x2)
        _route(x2, wr_ref[...], br_ref[...], eid_o, ew_o)

    i = pl.program_id(0)

    @pl.when(i < prompt_steps)
    def _():
        x1 = _mixer_out(x_ref[...], og_ref[...], r_ref[...], od_ref[...], gg_ref[...], wo_ref[...],
                        lg1_ref[...], lb1_ref[...])
        q = _mm(x1.astype(BF16), wq_ref[...]).astype(BF16)
        scale = MEM_DH ** -0.5
        outs = []
        for h in range(MEM_HEADS):
            hs = slice(h * MEM_DH, (h + 1) * MEM_DH)
            s = _mm_nt(q[:, hs], mk_ref[:, hs].astype(BF16)) * scale
            m = jnp.max(s, axis=-1, keepdims=True)
            p = jnp.exp(s - m)
            p = p / jnp.sum(p, axis=-1, keepdims=True)
            outs.append(_mm(p.astype(BF16), mv_ref[:, hs].astype(BF16)))
        tail(x1, jnp.concatenate(outs, axis=1).astype(BF16))

    @pl.when(i >= prompt_steps)
    def _():
        tail(x1s_ref[...], aos_ref[...])


def _post(x_p, og_p, r_p, od_p, mem_k, mem_v, x1_s, ao_s, gg, wo, lg1, lb1, wq, wmo, lg2, lb2, wr, br):
    n_p, n_s = x_p.shape[0], x1_s.shape[0]
    bsz, mt, _ = mem_k.shape
    tm = int(np.gcd(np.gcd(n_p // bsz, n_s), TOKEN_TILE))
    sp, ss = n_p // tm, n_s // tm
    steps_per_b = sp // bsz
    prow = lambda i: (jnp.minimum(i, sp - 1), 0)
    mrow = lambda i: (jnp.minimum(i, sp - 1) // steps_per_b, 0, 0)
    srow = lambda i: (jnp.maximum(i - sp, 0), 0)
    orow = lambda i: (i, 0)
    const = lambda i: (0, 0)
    full = lambda a: pl.BlockSpec(a.shape, const)
    n_total = n_p + n_s
    weights = (gg, wo, lg1, lb1, wq, wmo, lg2, lb2, wr, br)
    return pl.pallas_call(
        functools.partial(_post_body, prompt_steps=sp),
        out_shape=[jax.ShapeDtypeStruct((n_total, D_MODEL), F32),
                   jax.ShapeDtypeStruct((n_total * PAIR_TILES, LANES), jnp.uint32),
                   jax.ShapeDtypeStruct((n_total, LANES), jnp.int32),
                   jax.ShapeDtypeStruct((n_total, LANES), F32)],
        grid=(sp + ss,),
        in_specs=[pl.BlockSpec((tm, D_MODEL), prow), pl.BlockSpec((tm, GLA_VAL_W), prow),
                  pl.BlockSpec((tm, GLA_VAL_W), prow), pl.BlockSpec((tm, DIL_W), prow),
                  pl.BlockSpec((None, mt, D_MODEL), mrow), pl.BlockSpec((None, mt, D_MODEL), mrow),
                  pl.BlockSpec((tm, D_MODEL), srow), pl.BlockSpec((tm, D_MODEL), srow)]
                 + [full(w) for w in weights],
        out_specs=[pl.BlockSpec((tm, D_MODEL), orow), pl.BlockSpec((tm * PAIR_TILES, LANES), orow),
                   pl.BlockSpec((tm, LANES), orow), pl.BlockSpec((tm, LANES), orow)],
        compiler_params=_params(("arbitrary",)),
        name="post",
    )(x_p, og_p, r_p, od_p, mem_k, mem_v, x1_s, ao_s, *weights)


MOE_DMA_UNROLL = 16
MOE_BURSTS = 8
MXU_COLS = 256
MOE_VMEM_LIMIT = 60 * 1024 * 1024


def _moe_body(te_ref, nv_ref, nlive_ref, src0_ref, srcn_ref, dst0_ref, dstn_ref, spare_ref, xp_hbm,
              wg_ref, wu_ref, wd_ref, eo_hbm,
              xp, gbuf, obuf, src_smem, dst_smem, xsem, ssem, isem, dsem, wgb, wub, wdb, *, nt, spare0):
    t = pl.program_id(0)
    tile = MOE_TILE
    slot = t % 2
    rt = ROW_TILES
    pt = PAIR_TILES
    burst = tile // MOE_BURSTS

    def src_copy(idx_vmem, s):
        return pltpu.make_async_copy(idx_vmem.at[0], src_smem.at[s], isem.at[s])

    def dst_copy(idx_vmem, s):
        return pltpu.make_async_copy(idx_vmem.at[0], dst_smem.at[s], dsem.at[s % 2])

    def scatter_wait(s):
        pltpu.make_async_copy(obuf.at[s], eo_hbm.at[pl.ds(0, tile * rt)], ssem.at[s]).wait()

    def scatter_rows(s, d, lo, hi):
        def chunk(c, carry):
            for u in range(MOE_DMA_UNROLL):
                i = c * MOE_DMA_UNROLL + u
                row = pl.multiple_of(dst_smem[d, 0, i], rt)
                pltpu.make_async_copy(obuf.at[s, pl.ds(pl.multiple_of(i * rt, rt), rt)],
                                      eo_hbm.at[pl.ds(row, rt)], ssem.at[s]).start(priority=u % 2)
            return carry
        lax.fori_loop(lo // MOE_DMA_UNROLL, hi // MOE_DMA_UNROLL, chunk, 0)

    def live(i):
        return (i >= 0) & (i < nt) & (nv_ref[jnp.clip(i, 0, nt - 1)] > 0)

    @pl.when(t == 0)
    def _():
        xcp = pltpu.make_async_copy(xp_hbm, xp, xsem)
        xcp.start()
        obuf[...] = jnp.zeros_like(obuf)
        for s in range(2):
            cp = pltpu.make_async_copy(obuf.at[s], eo_hbm.at[pl.ds((spare0 + s * tile) * rt, tile * rt)], ssem.at[s])
            cp.start()
            cp.wait()
        for cp in (src_copy(src0_ref, 0), dst_copy(dst0_ref, 0), dst_copy(spare_ref, 2)):
            cp.start()
            cp.wait()
        xcp.wait()

    @pl.when(live(t + 1))
    def _():
        src_copy(srcn_ref, 1 - slot).start()
        dst_copy(dstn_ref, (t + 1) % 3).start()

    @pl.when((t == 0) | (te_ref[t] != te_ref[jnp.maximum(t - 1, 0)]))
    def _():
        wgb[...] = wg_ref[...].astype(BF16)
        wub[...] = wu_ref[...].astype(BF16)
        wdb[...] = wd_ref[...].astype(BF16)

    prev_dst = (t + 2) % 3
    nv_prev = jnp.where(t > 0, nv_ref[jnp.maximum(t - 1, 0)], 0)

    for s in range(2):
        @pl.when(live(t) & (slot == s))
        def _(s=s):
            def gather(i, carry):
                row = pl.multiple_of(src_smem[s, 0, i], pt)
                gbuf[pl.ds(pl.multiple_of(i * pt, pt), pt), :] = xp[pl.ds(row, pt), :]
                return carry
            lax.fori_loop(0, tile, gather, 0, unroll=16)

            bursts = iter(range(MOE_BURSTS))

            def burst_prev():
                b = next(bursts)
                lo = b * burst

                @pl.when(lo < nv_prev)
                def _():
                    for i in range(lo, lo + burst):
                        row = pl.multiple_of(dst_smem[prev_dst, 0, i], rt)
                        pltpu.make_async_copy(obuf.at[1 - s, pl.ds(i * rt, rt)], eo_hbm.at[pl.ds(row, rt)],
                                              ssem.at[1 - s]).start(priority=i % 2)

                @pl.when(lo >= nv_prev)
                def _():
                    row = pl.multiple_of(dst_smem[prev_dst, 0, lo], rt)
                    pltpu.make_async_copy(obuf.at[1 - s, pl.ds(lo * rt, burst * rt)],
                                          eo_hbm.at[pl.ds(row, burst * rt)], ssem.at[1 - s]).start()

            xb = _unpack_pairs(gbuf, tile)
            nc = MXU_COLS
            hs = []
            for c in range(EXPERT_HIDDEN // nc):
                cols = slice(c * nc, (c + 1) * nc)
                hg = _mm(xb, wgb[:, cols])
                burst_prev()
                hu = _mm(xb, wub[:, cols])
                burst_prev()
                hs.append((hg * jax.nn.sigmoid(hg) * hu).astype(BF16))
            h = jnp.concatenate(hs, axis=1)

            @pl.when(t >= 1)
            def _():
                scatter_wait(s)

            for c in range(D_MODEL // nc):
                out = _mm(h, wdb[:, c * nc:(c + 1) * nc])
                _to_token_tiles(obuf.at[s], out, col0=c * nc)
                burst_prev()

        @pl.when(live(t - 1) & jnp.logical_not(live(t)) & (slot == s))
        def _(s=s):
            scatter_rows(1 - s, prev_dst, 0, tile)

    @pl.when(live(t + 1))
    def _():
        src_copy(srcn_ref, 1 - slot).wait()
        dst_copy(dstn_ref, (t + 1) % 3).wait()

    @pl.when(t == nt - 1)
    def _():
        n_live = nlive_ref[0]

        @pl.when(n_live >= 1)
        def _():
            scatter_wait((n_live - 1) % 2)

        @pl.when(n_live >= 2)
        def _():
            scatter_wait(n_live % 2)


def _moe(tile_expert, n_valid, src, dst, x2p, wg, wu, wd, n_tok):
    nt = tile_expert.shape[0]
    tile = MOE_TILE
    rt = ROW_TILES
    n_live = jnp.sum((n_valid > 0).astype(jnp.int32)).reshape(1)
    spare0 = 2 * n_tok
    spare1 = ((spare0 + tile + jnp.arange(tile, dtype=jnp.int32)) * rt).reshape(1, 1, tile)
    wspec = lambda shape: pl.BlockSpec((None,) + shape, lambda t, te, nv, nl: (te[t], 0, 0))
    ispec = lambda f: pl.BlockSpec((1, 1, tile), lambda t, te, nv, nl: (f(t), 0, 0))
    first = lambda t: 0
    nxt = lambda t: jnp.minimum(t + 1, nt - 1)
    grid_spec = pltpu.PrefetchScalarGridSpec(
        num_scalar_prefetch=3,
        grid=(nt,),
        in_specs=[ispec(first), ispec(nxt), ispec(first), ispec(nxt), ispec(first),
                  pl.BlockSpec(memory_space=pl.ANY),
                  wspec((D_MODEL, EXPERT_HIDDEN)), wspec((D_MODEL, EXPERT_HIDDEN)),
                  wspec((EXPERT_HIDDEN, D_MODEL))],
        out_specs=pl.BlockSpec(memory_space=pl.ANY),
        scratch_shapes=[pltpu.VMEM(x2p.shape, jnp.uint32),
                        pltpu.VMEM((tile * PAIR_TILES, LANES), jnp.uint32),
                        pltpu.VMEM((2, tile * rt, LANES), F32),
                        pltpu.SMEM((2, 1, tile), jnp.int32),
                        pltpu.SMEM((3, 1, tile), jnp.int32),
                        pltpu.SemaphoreType.DMA,
                        pltpu.SemaphoreType.DMA((2,)),
                        pltpu.SemaphoreType.DMA((2,)),
                        pltpu.SemaphoreType.DMA((2,)),
                        pltpu.VMEM((D_MODEL, EXPERT_HIDDEN), BF16),
                        pltpu.VMEM((D_MODEL, EXPERT_HIDDEN), BF16),
                        pltpu.VMEM((EXPERT_HIDDEN, D_MODEL), BF16)])
    return pl.pallas_call(
        functools.partial(_moe_body, nt=nt, spare0=spare0),
        out_shape=jax.ShapeDtypeStruct(((spare0 + 2 * tile) * rt, LANES), F32),
        grid_spec=grid_spec,
        compiler_params=pltpu.CompilerParams(dimension_semantics=("arbitrary",), vmem_limit_bytes=MOE_VMEM_LIMIT),
        name="moe",
    )(tile_expert, n_valid, n_live, src, src, dst, dst, spare1, x2p, wg, wu, wd)


def _final_body(x_ref, ew_ref, e1_ref, e2_ref, lg_ref, lb_ref, y_ref):
    ew = ew_ref[...]
    n = x_ref.shape[0]
    moe = ew[:, 0:1] * _from_token_tiles(e1_ref, n) + ew[:, 1:2] * _from_token_tiles(e2_ref, n)
    y_ref[...] = _layer_norm(ALPHA * x_ref[...] + moe, lg_ref[...], lb_ref[...])


def _final(x2, ew, eo, lg, lb, row0, t):
    n_tok = x2.shape[0]
    tile = int(np.gcd(np.gcd(row0, t), np.gcd(n_tok, TOKEN_TILE)))
    b0 = row0 // tile
    b1 = n_tok // tile
    const = lambda i: (0, 0)
    blk = lambda off: pl.BlockSpec((tile, D_MODEL), lambda i: (off + i, 0))
    tblk = lambda off: pl.BlockSpec((tile * ROW_TILES, LANES), lambda i: (off + i, 0))
    return pl.pallas_call(
        _final_body,
        out_shape=jax.ShapeDtypeStruct((t, D_MODEL), F32),
        grid=(t // tile,),
        in_specs=[blk(b0), pl.BlockSpec((tile, LANES), lambda i: (b0 + i, 0)), tblk(b0), tblk(b1 + b0),
                  pl.BlockSpec(lg.shape, const), pl.BlockSpec(lb.shape, const)],
        out_specs=blk(0),
        compiler_params=_params(("arbitrary",)),
        name="final",
    )(x2, ew, eo, eo, lg, lb)


def _routing_tables(eid, n_tok):
    tile = MOE_TILE
    n_assign = 2 * n_tok
    assert n_assign % tile == 0
    nt = n_assign // tile + N_EXPERTS + 1
    flat = eid.reshape(-1).astype(jnp.int32)
    experts = jnp.arange(N_EXPERTS, dtype=jnp.int32)
    counts = jnp.sum((flat[:, None] == experts[None, :]).astype(jnp.int32), axis=0)
    pad = (-counts) % tile
    unused = 2 * N_EXPERTS
    pad_keys = jnp.where(jnp.arange(tile, dtype=jnp.int32)[None, :] < pad[:, None],
                         2 * experts[:, None] + 1, unused).reshape(-1)
    keys = jnp.concatenate([2 * flat, pad_keys, jnp.full((tile,), unused, jnp.int32)])
    vals = jnp.concatenate([jnp.arange(n_assign, dtype=jnp.int32),
                            jnp.full(((N_EXPERTS + 1) * tile,), -1, jnp.int32)])
    keys, vals = lax.sort((keys, vals), num_keys=1)
    keys = keys.reshape(nt, tile)
    vals = vals.reshape(nt, tile)
    tile_expert = jnp.minimum(keys[:, 0] // 2, N_EXPERTS - 1)
    real = vals >= 0
    n_valid = jnp.sum(real.astype(jnp.int32), axis=1)
    a = jnp.maximum(vals, 0)
    src = a >> 1
    spare = 2 * n_tok + (jnp.arange(nt, dtype=jnp.int32)[:, None] % 2) * tile + jnp.arange(tile, dtype=jnp.int32)[None, :]
    dst = jnp.where(real, (a & 1) * n_tok + (a >> 1), spare)
    return tile_expert, n_valid, src.reshape(nt, 1, tile), dst.reshape(nt, 1, tile)


def kernel(x_prompt, x_sample, mem_prompt, cache_dil_k, cache_dil_v, state_gla, cache_mem_k, cache_mem_v, w_in, w_gate_lr, b_gate, g_gla_norm, w_out, ln_mix_g, ln_mix_b, w_mem_q, w_mem_k, w_mem_v, w_mem_o, ln_mem_g, ln_mem_b, w_route_group, b_route_group, w_route_expert, b_route_expert, w_exp_gate, w_exp_up, w_exp_down, ln_ffn_g, ln_ffn_b):
    assert w_in.shape[0] == DEPTH
    bp, seq, _ = x_prompt.shape
    db, tdec, _ = x_sample.shape
    buf = cache_dil_k.shape[2]
    mt = mem_prompt.shape[1]
    n_p, n_s = bp * seq, db * tdec
    n_tok = n_p + n_s
    l = 0

    a0 = 2 * GLA_KEY_W + 2 * GLA_VAL_W
    w_main = jnp.concatenate([w_in[l][:, :a0], w_in[l][:, a0 + GATE_RANK:]], axis=1).astype(BF16)
    w_a = jnp.pad(w_in[l][:, a0:a0 + GATE_RANK], ((0, 0), (0, LANES - GATE_RANK))).astype(BF16)
    w_gl = jnp.pad(w_gate_lr[l], ((0, LANES - GATE_RANK), (0, 0))).astype(BF16)
    b_g = b_gate[l][None, :]
    row1 = lambda a: a[l][None, :]
    w_o = w_out[l].astype(BF16)
    w_q = w_mem_q[l].astype(BF16)
    w_k = w_mem_k[l].astype(BF16)
    w_v = w_mem_v[l].astype(BF16)
    w_mo = w_mem_o[l].astype(BF16)
    w_r = jnp.pad(jnp.concatenate([w_route_expert[l], w_route_group[l]], axis=1),
                  ((0, 0), (0, LANES - N_EXPERTS - N_GROUPS))).astype(BF16)
    b_r = jnp.pad(jnp.concatenate([b_route_expert[l], b_route_group[l]]), (0, LANES - N_EXPERTS - N_GROUPS))[None, :]

    def mixer_inputs(x2d, seq_len, keep, q_rows):
        return _proj(x2d, w_main, w_a, w_gl, b_g, seq=seq_len, keep=keep, q_rows=q_rows)

    def post_a(x2d, o_g, r_g, o_d):
        return _post_a(x2d, o_g, r_g, o_d, row1(g_gla_norm), w_o, row1(ln_mix_g), row1(ln_mix_b), w_q)

    xp = x_prompt.reshape(n_p, D_MODEL)
    keep = min(WINDOW_MAX, seq)
    qg, kg, vg, rg, la, qd, kd, vd, kd_rows, vd_rows = mixer_inputs(xp, seq, keep, False)
    sh = lambda t: t.reshape(bp, seq, t.shape[-1])
    s0 = jnp.zeros((bp, H_GLA, GLA_DK, GLA_DV), F32)
    o_g, s_fin_p = _gla(sh(qg), sh(kg), sh(vg), sh(la), s0, chain=True)
    o_d = _dil_prompt(sh(qd), sh(kd), sh(vd))
    mem_k, mem_v = _mem_kv(mem_prompt, w_k, w_v)
    prompt_mix = (xp, o_g.reshape(n_p, -1), rg, o_d.reshape(n_p, -1))
    dk_p = kd_rows.reshape(1, bp, keep, H_DIL, DIL_DH)
    dv_p = vd_rows.reshape(1, bp, keep, H_DIL, DIL_DH)

    xs = x_sample.reshape(n_s, D_MODEL)
    qg, kg, vg, rg, la, _, _, _, kd_rows, vd_rows, qd_rows = mixer_inputs(xs, tdec, tdec, True)
    shs = lambda t: t.reshape(db, tdec, t.shape[-1])
    o_g, s_new = _gla(shs(qg), shs(kg), shs(vg), shs(la), state_gla[l], chain=False)
    head_rows = lambda t: t.reshape(db, tdec * H_DIL, DIL_DH)
    groups = lambda c: c[l].reshape(db, buf // DIL_GROUP, DIL_GROUP * H_DIL, DIL_DH)
    nk, nv, o_d = _dil_sample(head_rows(qd_rows), head_rows(kd_rows), head_rows(vd_rows),
                              groups(cache_dil_k), groups(cache_dil_v))
    x1_s, qm = post_a(xs, o_g.reshape(n_s, -1), rg, o_d.reshape(n_s, DIL_W))
    gm = 8 if db % 8 == 0 else 1

    def cache_rows(c):
        return (c[l].reshape(db, mt, MEM_HEADS, MEM_HALVES, LANES).transpose(0, 1, 3, 2, 4)
                .reshape(db, mt * MEM_HALVES * MEM_HEADS, LANES))

    q_rows = (qm.reshape(db, tdec, MEM_HEADS, MEM_HALVES, LANES).transpose(0, 3, 1, 2, 4)
              .reshape(db, MEM_HALVES * tdec * MEM_HEADS, LANES))
    ao_s = _mem_attn_rows(q_rows, cache_rows(cache_mem_k), cache_rows(cache_mem_v), g=gm)
    ao_s = (ao_s.reshape(db, MEM_HALVES, tdec, MEM_HEADS, LANES).transpose(0, 2, 3, 1, 4).reshape(n_s, D_MODEL))

    x2, x2p, eid, ew = _post(*prompt_mix, mem_k, mem_v, x1_s, ao_s, row1(g_gla_norm), w_o, row1(ln_mix_g),
                             row1(ln_mix_b), w_q, w_mo, row1(ln_mem_g), row1(ln_mem_b), w_r, b_r)
    tile_expert, n_valid, src, dst = _routing_tables(eid[:, :2], n_tok)
    eo = _moe(tile_expert, n_valid, src * PAIR_TILES, dst * ROW_TILES, x2p,
              w_exp_gate[l], w_exp_up[l], w_exp_down[l], n_tok)
    y_p = _final(x2, ew, eo, row1(ln_ffn_g), row1(ln_ffn_b), 0, n_p)
    y_s = _final(x2, ew, eo, row1(ln_ffn_g), row1(ln_ffn_b), n_p, n_s)

    return (y_p.reshape(bp, seq, D_MODEL), y_s.reshape(db, tdec, D_MODEL),
            dk_p, dv_p, s_fin_p[None], mem_k.reshape(1, bp, mt, MEM_HEADS, MEM_DH),
            mem_v.reshape(1, bp, mt, MEM_HEADS, MEM_DH),
            nk.reshape(1, db, buf, H_DIL, DIL_DH), nv.reshape(1, db, buf, H_DIL, DIL_DH), s_new[None])
```

```python
import functools

import numpy as np
import jax
import jax.numpy as jnp
from jax import lax
from jax.experimental import pallas as pl
from jax.experimental.pallas import tpu as pltpu

F32 = jnp.float32
BF16 = jnp.bfloat16

D_MODEL = 1024
H_GLA = 4
GLA_DK = 64
GLA_DV = 128
GLA_KEY_W = H_GLA * GLA_DK
GLA_VAL_W = H_GLA * GLA_DV
GATE_RANK = 16
GATE_TAU = 16.0
GLA_CHUNK = 64
H_DIL = 4
DIL_DH = 128
DIL_W = H_DIL * DIL_DH
DIL_PAIRS = ((128, 1), (512, 4), (2048, 16))
DIL_SPAN = 128
WINDOW_MAX = 2048
MEM_HEADS = 4
MEM_DH = 256
N_GROUPS = 4
EXPERTS_PER_GROUP = 8
N_EXPERTS = N_GROUPS * EXPERTS_PER_GROUP
EXPERT_HIDDEN = 512
DEPTH = 1
ALPHA = (2.0 * DEPTH) ** 0.25
LN_EPS = 1e-5
NEG_INF = -1e30

LANES = 128
VMEM_LIMIT = 56 * 1024 * 1024

TOKEN_TILE = 512
MOE_TILE = 256


def _mm(a, b):
    return jnp.dot(a, b, preferred_element_type=F32)


def _mm_nt(a, b):
    return lax.dot_general(a, b, (((1,), (1,)), ((), ())), preferred_element_type=F32)


def _mm_tn(a, b):
    return lax.dot_general(a, b, (((0,), (0,)), ((), ())), preferred_element_type=F32)


def _layer_norm(v, g, b):
    mu = jnp.mean(v, axis=-1, keepdims=True)
    d = v - mu
    var = jnp.mean(d * d, axis=-1, keepdims=True)
    return d * lax.rsqrt(var + LN_EPS) * g + b


def _params(sem):
    return pltpu.CompilerParams(dimension_semantics=sem, vmem_limit_bytes=VMEM_LIMIT)


ROW_TILES = D_MODEL // LANES


def _to_token_tiles(ref, val, col0=0):
    n = val.shape[0]
    for s in range(val.shape[1] // LANES):
        ref[pl.ds(col0 // LANES + s, n, stride=ROW_TILES), :] = val[:, s * LANES:(s + 1) * LANES]


def _from_token_tiles(ref, n):
    return jnp.concatenate([ref[pl.ds(s, n, stride=ROW_TILES), :] for s in range(ROW_TILES)], axis=1)


PAIR_TILES = D_MODEL // 2 // LANES


def _pack_pairs(ref, val):
    n = val.shape[0]
    half = D_MODEL // 2
    bits = lambda v: pltpu.bitcast(v.astype(BF16).astype(F32), jnp.uint32)
    words = (bits(val[:, :half]) >> 16) | (bits(val[:, half:]) & jnp.uint32(0xFFFF0000))
    for s in range(PAIR_TILES):
        ref[pl.ds(s, n, stride=PAIR_TILES), :] = words[:, s * LANES:(s + 1) * LANES]


def _unpack_pairs(ref, n):
    words = [ref[pl.ds(s, n, stride=PAIR_TILES), :] for s in range(PAIR_TILES)]
    lo = [pltpu.bitcast(w << 16, F32).astype(BF16) for w in words]
    hi = [pltpu.bitcast(w & jnp.uint32(0xFFFF0000), F32).astype(BF16) for w in words]
    return jnp.concatenate(lo + hi, axis=1)


def _to_head_rows(ref, val):
    n = val.shape[0]
    for h in range(H_DIL):
        ref[pl.ds(h, n, stride=H_DIL), :] = val[:, h * DIL_DH:(h + 1) * DIL_DH]


def _proj_body(x_ref, w_ref, wa_ref, wgl_ref, bg_ref,
               qg_o, kg_o, vg_o, r_o, la_o, qd_o, kd_o, vd_o, *row_outs):
    xb = x_ref[...].astype(BF16)

    def mm(lo, hi):
        return _mm(xb, w_ref[:, lo:hi])

    qg_o[...] = mm(0, 256) * (GLA_DK ** -0.5)
    kg_o[...] = mm(256, 512)
    vg_o[...] = mm(512, 1024)
    r_o[...] = mm(1024, 1536)
    qd = mm(1536, 2048)
    kd = mm(2048, 2560)
    vd = mm(2560, 3072)
    qd_o[...] = qd
    kd_o[...] = kd
    vd_o[...] = vd
    for ref, val in zip(row_outs, (kd, vd, qd)):
        _to_head_rows(ref, val)
    a_lr = _mm(xb, wa_ref[...])
    z = _mm(a_lr.astype(BF16), wgl_ref[...]) + bg_ref[...]
    la_o[...] = (jnp.minimum(z, 0.0) - jnp.log1p(jnp.exp(-jnp.abs(z)))) * (1.0 / GATE_TAU)


def _proj(x, w_main, w_a, w_gl, b_g, *, seq, keep, q_rows):
    t = x.shape[0]
    tm = min(TOKEN_TILE, t)
    widths = (256, 256, 512, 512, 256, 512, 512, 512)
    row = lambda i: (i, 0)
    const = lambda i: (0, 0)
    if keep == seq:
        kept = row
    else:
        assert seq % tm == 0 and keep % tm == 0
        per_seq, per_keep = seq // tm, keep // tm
        kept = lambda i: ((i // per_seq) * per_keep + jnp.maximum(i % per_seq - (per_seq - per_keep), 0), 0)
    n_rows = 3 if q_rows else 2
    n_kept = t // seq * keep
    return pl.pallas_call(
        _proj_body,
        out_shape=[jax.ShapeDtypeStruct((t, w), F32) for w in widths]
                  + [jax.ShapeDtypeStruct((n_kept * H_DIL, DIL_DH), F32)] * n_rows,
        grid=(t // tm,),
        in_specs=[pl.BlockSpec((tm, D_MODEL), row),
                  pl.BlockSpec(w_main.shape, const),
                  pl.BlockSpec(w_a.shape, const),
                  pl.BlockSpec(w_gl.shape, const),
                  pl.BlockSpec(b_g.shape, const)],
        out_specs=[pl.BlockSpec((tm, w), row) for w in widths]
                  + [pl.BlockSpec((tm * H_DIL, DIL_DH), kept)] * n_rows,
        compiler_params=_params(("arbitrary",)),
        name="proj",
    )(x, w_main, w_a, w_gl, b_g)


GLA_ROWS = 256


def _split3(a):
    a1 = a.astype(BF16)
    r1 = a - a1.astype(F32)
    a2 = r1.astype(BF16)
    r2 = r1 - a2.astype(F32)
    return a1, a2, r2.astype(BF16)


def _gla_body(q_ref, k_ref, v_ref, la_ref, s0_ref, o_ref, sfin_ref, s_scr, *, rows, blk, chain):
    nb = rows // blk
    j = pl.program_id(1)
    if chain:
        @pl.when(j == 0)
        def _():
            s_scr[...] = s0_ref[0]

    ri = lax.broadcasted_iota(jnp.int32, (rows, rows), 0)
    ci = lax.broadcasted_iota(jnp.int32, (rows, rows), 1)
    causal = (ci <= ri) & (ri // blk == ci // blk)
    tri = jnp.where(causal, 1.0, 0.0).astype(BF16)
    rblk = lax.broadcasted_iota(jnp.int32, (rows, nb * GLA_DV), 0) // blk
    cblk = lax.broadcasted_iota(jnp.int32, (rows, nb * GLA_DV), 1) // GLA_DV
    own = rblk == cblk

    q = q_ref[...].reshape(rows, GLA_KEY_W)
    k = k_ref[...].reshape(rows, GLA_KEY_W)
    v = v_ref[...].reshape(rows, GLA_VAL_W)
    a = la_ref[...].reshape(rows, GLA_KEY_W)

    cum = _mm(tri, jnp.concatenate(_split3(a), axis=1))
    b = cum[:, :GLA_KEY_W] + cum[:, GLA_KEY_W:2 * GLA_KEY_W] + cum[:, 2 * GLA_KEY_W:]
    b_last = jnp.concatenate([b[(i + 1) * blk - 1:(i + 1) * blk] for i in range(nb)], axis=0)
    b_end = jnp.concatenate([jnp.broadcast_to(b_last[i:i + 1], (blk, GLA_KEY_W)) for i in range(nb)], axis=0)
    q_t = (q * jnp.exp(b)).astype(BF16)
    k_t = (k * jnp.exp(-b)).astype(BF16)
    k_end = (k * jnp.exp(b_end - b)).astype(BF16)
    vb = v.astype(BF16)
    dec = jnp.transpose(jnp.concatenate([jnp.exp(b_last), jnp.zeros((LANES - nb, GLA_KEY_W), F32)], axis=0))

    outs = []
    for h in range(H_GLA):
        hk = slice(h * GLA_DK, (h + 1) * GLA_DK)
        vh = vb[:, h * GLA_DV:(h + 1) * GLA_DV]
        a_in = jnp.where(causal, _mm_nt(q_t[:, hk], k_t[:, hk]), 0.0).astype(BF16)
        o_h = _mm(a_in, vh)
        v_exp = jnp.where(own, jnp.concatenate([vh] * nb, axis=1), jnp.zeros((), BF16))
        kv = _mm_tn(k_end[:, hk], v_exp)
        states = []
        for i in range(nb):
            if chain:
                s_i = s_scr[h] if i == 0 else s_next
            else:
                s_i = s0_ref[i, h]
            states.append(s_i.astype(BF16))
            d_i = jnp.broadcast_to(dec[hk, i:i + 1], (GLA_DK, GLA_DV))
            s_next = d_i * s_i + kv[:, i * GLA_DV:(i + 1) * GLA_DV]
            if not chain:
                sfin_ref[i, h] = s_next
        if chain:
            s_scr[h] = s_next
        inter = _mm(q_t[:, hk], jnp.concatenate(states, axis=1))
        inter = jnp.where(own, inter, 0.0)
        for i in range(nb):
            o_h = o_h + inter[:, i * GLA_DV:(i + 1) * GLA_DV]
        outs.append(o_h)
    o_ref[...] = jnp.concatenate(outs, axis=1).reshape(o_ref.shape)

    if chain:
        @pl.when(j == pl.num_programs(1) - 1)
        def _():
            sfin_ref[0] = s_scr[...]


def _gla(q, k, v, la, s0, *, chain):
    nbat, s, _ = q.shape
    if chain:
        bb, rows, blk = 1, min(GLA_ROWS, s), GLA_CHUNK
        assert s % rows == 0 and rows % blk == 0
    else:
        bb, rows, blk = GLA_CHUNK // s, s, s
        assert GLA_CHUNK % s == 0 and nbat % bb == 0
    tok = lambda i, j: (i, j, 0)
    st = lambda i, j: (i, 0, 0, 0)
    body = functools.partial(_gla_body, rows=bb * rows, blk=blk, chain=chain)
    return pl.pallas_call(
        body,
        out_shape=[jax.ShapeDtypeStruct((nbat, s, GLA_VAL_W), F32),
                   jax.ShapeDtypeStruct((nbat, H_GLA, GLA_DK, GLA_DV), F32)],
        grid=(nbat // bb, s // rows),
        in_specs=[pl.BlockSpec((bb, rows, GLA_KEY_W), tok),
                  pl.BlockSpec((bb, rows, GLA_KEY_W), tok),
                  pl.BlockSpec((bb, rows, GLA_VAL_W), tok),
                  pl.BlockSpec((bb, rows, GLA_KEY_W), tok),
                  pl.BlockSpec((bb, H_GLA, GLA_DK, GLA_DV), st)],
        out_specs=[pl.BlockSpec((bb, rows, GLA_VAL_W), tok),
                   pl.BlockSpec((bb, H_GLA, GLA_DK, GLA_DV), st)],
        scratch_shapes=[pltpu.VMEM((H_GLA, GLA_DK, GLA_DV), F32)],
        compiler_params=_params(("arbitrary", "arbitrary")),
        name="gla",
    )(q, k, v, la, s0)


DIL_QBLK = 128
DIL_TILE = 2048
DIL_UNROLL = 16


def _dil_body(q_ref, k_ref, v_ref, o_ref, acc_ref, m_ref, l_ref, *, tile):
    blk = DIL_QBLK
    t0 = pl.program_id(2) * tile
    qscale = (DIL_DH ** -0.5) * float(np.log2(np.e))
    ri = lax.broadcasted_iota(jnp.int32, (blk, 2 * blk), 0)
    ci = lax.broadcasted_iota(jnp.int32, (blk, 2 * blk), 1)
    band_bias = jnp.where((ci >= ri) & (ci <= ri + DIL_SPAN), 0.0, NEG_INF)
    prev_cols = lax.broadcasted_iota(jnp.int32, (1, 2 * blk), 1) < blk

    def rows(start, dil):
        return pl.ds(start, blk) if dil == 1 else pl.ds(start, blk, stride=dil)

    def block(dil, q0, first, last):
        qr = rows(q0, dil)
        cur0 = t0 + q0
        prev0 = cur0 - dil * blk
        prev_bias = jnp.where(prev_cols & (prev0 < 0), NEG_INF, 0.0)
        prev0 = jnp.maximum(prev0, 0)
        kw = jnp.concatenate([k_ref[rows(prev0, dil), :], k_ref[rows(cur0, dil), :]], axis=0)
        vw = jnp.concatenate([v_ref[rows(prev0, dil), :], v_ref[rows(cur0, dil), :]], axis=0)
        s = _mm_nt((q_ref[qr, :] * qscale).astype(BF16), kw.astype(BF16)) + band_bias + prev_bias
        m_blk = jnp.max(s, axis=-1, keepdims=True)
        p = jnp.exp2(s - m_blk)
        l_blk = jnp.sum(p, axis=-1, keepdims=True)
        acc = _mm(p.astype(BF16), vw.astype(BF16))
        m_new = jnp.broadcast_to(m_blk, (blk, LANES))
        l_new = jnp.broadcast_to(l_blk, (blk, LANES))
        if not first:
            m_old = m_ref[qr, :]
            m_new = jnp.maximum(m_old, m_new)
            a_old = jnp.exp2(m_old - m_new)
            a_blk = jnp.exp2(m_blk - m_new)
            l_new = a_old * l_ref[qr, :] + a_blk * l_blk
            acc = a_old * acc_ref[qr, :] + a_blk * acc
        if last:
            o_ref[qr, :] = acc / l_new
        else:
            acc_ref[qr, :] = acc
            m_ref[qr, :] = m_new
            l_ref[qr, :] = l_new

    dils = sorted((d for _, d in DIL_PAIRS), reverse=True)
    for bi, dil in enumerate(dils):
        per_res = tile // (dil * blk)

        def body(i, carry, dil=dil, per_res=per_res, bi=bi):
            q0 = (i % dil) + (i // dil) * (dil * blk) if dil > 1 else pl.multiple_of(i * blk, blk)
            block(dil, q0, bi == 0, bi == len(dils) - 1)
            return carry

        lax.fori_loop(0, dil * per_res, body, 0, unroll=DIL_UNROLL)


def _dil_prompt(q, k, v):
    bsz, s, _ = q.shape
    tile = min(DIL_TILE, s)
    assert s % tile == 0 and all(tile % (d * DIL_QBLK) == 0 for _, d in DIL_PAIRS)
    assert all(w // d == DIL_SPAN for w, d in DIL_PAIRS)
    qs = pl.BlockSpec((None, tile, DIL_DH), lambda b, h, n: (b, n, h))
    ks = pl.BlockSpec((None, s, DIL_DH), lambda b, h, n: (b, 0, h))
    return pl.pallas_call(
        functools.partial(_dil_body, tile=tile),
        out_shape=jax.ShapeDtypeStruct((bsz, s, DIL_W), F32),
        grid=(bsz, H_DIL, s // tile),
        in_specs=[qs, ks, ks],
        out_specs=qs,
        scratch_shapes=[pltpu.VMEM((tile, DIL_DH), F32), pltpu.VMEM((tile, LANES), F32),
                        pltpu.VMEM((tile, LANES), F32)],
        compiler_params=_params(("arbitrary", "arbitrary", "arbitrary")),
        name="dil_prompt",
    )(q, k, v)


DIL_GROUP = 16
DIL_TAIL = 512


def _dil_sample_counts(buf, t_new):
    ga = (buf - DIL_TAIL) // DIL_GROUP
    pos_all = np.arange(buf)
    sel = (pos_all >= ga * DIL_GROUP) | (pos_all % DIL_GROUP < t_new)
    t = np.arange(t_new)

    def count(pos, real):
        cnt = np.zeros((t_new, pos.size), np.float32)
        for window, dil in DIL_PAIRS:
            delta = buf + t[:, None] - pos[None, :]
            cnt += (delta >= 0) & (delta % dil == 0) & (delta // dil <= window // dil) & real[None, :]
        return cnt

    assert not count(pos_all[~sel], np.ones((~sel).sum(), bool)).any()
    pad_new = LANES // H_DIL
    pos = np.concatenate([pos_all[sel], buf + np.arange(pad_new)])
    real = np.concatenate([np.ones(sel.sum(), bool), np.arange(pad_new) < t_new])
    cnt = count(pos, real)
    same_head = np.eye(H_DIL, dtype=np.float32)
    full = cnt[:, None, :, None] * same_head[None, :, None, :]
    return full.reshape(t_new * H_DIL, pos.size * H_DIL)


def _dil_sample_body(q_ref, kn_ref, vn_ref, ck_ref, cv_ref, cnt_ref, nk_ref, nv_ref, o_ref, *, ga):
    g = ck_ref.shape[0]
    half = ck_ref.shape[1] // 2
    k_new = kn_ref[...]
    v_new = vn_ref[...]
    for c_ref, n_ref, new in ((ck_ref, nk_ref, k_new), (cv_ref, nv_ref, v_new)):
        n_ref[:, 0:half, :] = c_ref[:, half:, :]
        n_ref[0:g - 1, half:, :] = c_ref[1:g, 0:half, :]
        n_ref[g - 1, half:, :] = new

    zpad = jnp.zeros((LANES - half, DIL_DH), F32)

    def keys(c_ref, new):
        main = c_ref[0:ga, 0:half, :].reshape(ga * half, DIL_DH)
        tail = c_ref[ga:g, :, :].reshape((g - ga) * 2 * half, DIL_DH)
        return jnp.concatenate([main, tail, new, zpad], axis=0).astype(BF16)

    s = _mm_nt(q_ref[...].astype(BF16), keys(ck_ref, k_new)) * (DIL_DH ** -0.5)
    cnt = cnt_ref[...]
    s = jnp.where(cnt > 0.0, s, NEG_INF)
    m = jnp.max(s, axis=-1, keepdims=True)
    p = jnp.exp(s - m) * cnt
    den = jnp.sum(p, axis=-1, keepdims=True)
    o_ref[...] = _mm(p.astype(BF16), keys(cv_ref, v_new)) / den


def _dil_sample(q, k_new, v_new, cache_k, cache_v):
    db, rows_new, _ = q.shape
    g, grows = cache_k.shape[1], cache_k.shape[2]
    assert grows == DIL_GROUP * H_DIL and rows_new * 2 == grows
    buf = g * DIL_GROUP
    ga = (buf - DIL_TAIL) // DIL_GROUP
    cnt = jnp.asarray(_dil_sample_counts(buf, rows_new // H_DIL))
    new = pl.BlockSpec((None, rows_new, DIL_DH), lambda b: (b, 0, 0))
    big = pl.BlockSpec((None, g, grows, DIL_DH), lambda b: (b, 0, 0, 0))
    return pl.pallas_call(
        functools.partial(_dil_sample_body, ga=ga),
        out_shape=[jax.ShapeDtypeStruct(cache_k.shape, F32),
                   jax.ShapeDtypeStruct(cache_v.shape, F32),
                   jax.ShapeDtypeStruct((db, rows_new, DIL_DH), F32)],
        grid=(db,),
        in_specs=[new, new, new, big, big, pl.BlockSpec(cnt.shape, lambda b: (0, 0))],
        out_specs=[big, big, new],
        compiler_params=_params(("arbitrary",)),
        name="dil_sample",
    )(q, k_new, v_new, cache_k, cache_v, cnt)


def _post_a_body(x_ref, og_ref, r_ref, od_ref, gg_ref, wo_ref, lg_ref, lb_ref, wq_ref, x1_o, qm_o):
    x1 = _mixer_out(x_ref[...], og_ref[...], r_ref[...], od_ref[...], gg_ref[...], wo_ref[...],
                    lg_ref[...], lb_ref[...])
    x1_o[...] = x1
    qm_o[...] = _mm(x1.astype(BF16), wq_ref[...]).astype(BF16)


def _post_a(x, og, r, od, gg, wo, lg, lb, wq):
    t = x.shape[0]
    tm = min(TOKEN_TILE, t)
    row = lambda i: (i, 0)
    const = lambda i: (0, 0)
    full = lambda a: pl.BlockSpec(a.shape, const)
    return pl.pallas_call(
        _post_a_body,
        out_shape=[jax.ShapeDtypeStruct((t, D_MODEL), F32), jax.ShapeDtypeStruct((t, D_MODEL), BF16)],
        grid=(t // tm,),
        in_specs=[pl.BlockSpec((tm, D_MODEL), row), pl.BlockSpec((tm, GLA_VAL_W), row),
                  pl.BlockSpec((tm, GLA_VAL_W), row), pl.BlockSpec((tm, DIL_W), row),
                  full(gg), full(wo), full(lg), full(lb), full(wq)],
        out_specs=[pl.BlockSpec((tm, D_MODEL), row), pl.BlockSpec((tm, D_MODEL), row)],
        compiler_params=_params(("arbitrary",)),
        name="post_a",
    )(x, og, r, od, gg, wo, lg, lb, wq)


def _mem_kv_body(m_ref, wk_ref, wv_ref, k_o, v_o):
    mb = m_ref[...].astype(BF16)
    k_o[...] = _mm(mb, wk_ref[...])
    v_o[...] = _mm(mb, wv_ref[...])


def _mem_kv(mem, wk, wv):
    bsz, mt, _ = mem.shape
    blk = pl.BlockSpec((None, mt, D_MODEL), lambda b: (b, 0, 0))
    w = pl.BlockSpec((D_MODEL, D_MODEL), lambda b: (0, 0))
    return pl.pallas_call(
        _mem_kv_body,
        out_shape=[jax.ShapeDtypeStruct((bsz, mt, D_MODEL), F32)] * 2,
        grid=(bsz,),
        in_specs=[blk, w, w],
        out_specs=[blk, blk],
        compiler_params=_params(("arbitrary",)),
        name="mem_kv",
    )(mem, wk, wv)


MEM_HALVES = MEM_DH // LANES


def _mem_attn_rows_body(q_ref, k_ref, v_ref, mask_ref, o_ref, *, g):
    scale = MEM_DH ** -0.5
    hh = MEM_HEADS
    nq = q_ref.shape[1] // MEM_HALVES
    nk = k_ref.shape[1]
    valid = mask_ref[...] > 0.0
    for ig in range(g):
        part = _mm_nt(q_ref[ig], k_ref[ig].astype(BF16))
        s = part[0:nq] + pltpu.roll(part[nq:2 * nq], nk - hh, 1)
        s = jnp.where(valid, s * scale, NEG_INF)
        m = jnp.max(s, axis=-1, keepdims=True)
        p = jnp.exp(s - m)
        p = (p / jnp.sum(p, axis=-1, keepdims=True)).astype(F32)
        both = jnp.concatenate([p, pltpu.roll(p, hh, 1)], axis=0).astype(BF16)
        o_ref[ig] = _mm(both, v_ref[ig].astype(BF16)).astype(BF16)


def _mem_attn_rows(q, mk, mv, *, g):
    nb, qr, _ = q.shape
    kr = mk.shape[1]
    nq = qr // MEM_HALVES
    col = np.arange(kr)
    mask = ((col[None, :] % MEM_HEADS == np.arange(nq)[:, None] % MEM_HEADS)
            & (col[None, :] % (MEM_HEADS * MEM_HALVES) < MEM_HEADS)).astype(np.float32)
    qs = pl.BlockSpec((g, qr, LANES), lambda i: (i, 0, 0))
    ks = pl.BlockSpec((g, kr, LANES), lambda i: (i, 0, 0))
    return pl.pallas_call(
        functools.partial(_mem_attn_rows_body, g=g),
        out_shape=jax.ShapeDtypeStruct((nb, qr, LANES), BF16),
        grid=(nb // g,),
        in_specs=[qs, ks, ks, pl.BlockSpec(mask.shape, lambda i: (0, 0))],
        out_specs=qs,
        compiler_params=_params(("arbitrary",)),
        name="mem_attn_rows",
    )(q, mk, mv, jnp.asarray(mask))


ROUTE_GROUP_LANE0 = N_EXPERTS


def _mixer_out(x, og, r, od, gg, wo, lg, lb):
    parts = []
    for h in range(H_GLA):
        oh = og[:, h * GLA_DV:(h + 1) * GLA_DV]
        parts.append(oh * lax.rsqrt(jnp.mean(oh * oh, axis=-1, keepdims=True) + LN_EPS))
    on = jnp.concatenate(parts, axis=1) * gg * (r * jax.nn.sigmoid(r))
    cat = jnp.concatenate([on.astype(BF16), od.astype(BF16)], axis=1)
    return _layer_norm(ALPHA * x + _mm(cat, wo), lg, lb)


def _route(x2, wr, br, eid_o, ew_o):
    logits = _mm(x2.astype(BF16), wr) + br
    lane = lax.broadcasted_iota(jnp.int32, logits.shape, 1).astype(F32)
    big = float(LANES)

    def first_argmax(vals, vmax):
        return jnp.min(jnp.where(vals == vmax, lane, big), axis=-1, keepdims=True)

    is_g = (lane >= ROUTE_GROUP_LANE0) & (lane < ROUTE_GROUP_LANE0 + N_GROUPS)
    gl = jnp.where(is_g, logits, NEG_INF)
    gmax = jnp.max(gl, axis=-1, keepdims=True)
    g_sel = first_argmax(gl, gmax) - ROUTE_GROUP_LANE0
    g_w = 1.0 / jnp.sum(jnp.where(is_g, jnp.exp(gl - gmax), 0.0), axis=-1, keepdims=True)
    lo = g_sel * EXPERTS_PER_GROUP
    el = jnp.where((lane >= lo) & (lane < lo + EXPERTS_PER_GROUP), logits, NEG_INF)
    v1 = jnp.max(el, axis=-1, keepdims=True)
    i1 = first_argmax(el, v1)
    el2 = jnp.where(lane == i1, NEG_INF, el)
    v2 = jnp.max(el2, axis=-1, keepdims=True)
    i2 = first_argmax(el2, v2)
    e = jnp.exp(v2 - v1)
    w1 = g_w / (1.0 + e)
    w2 = g_w * e / (1.0 + e)
    eid_o[...] = jnp.where(lane == 0.0, i1, jnp.where(lane == 1.0, i2, 0.0)).astype(jnp.int32)
    ew_o[...] = jnp.where(lane == 0.0, w1, jnp.where(lane == 1.0, w2, 0.0))


def _post_body(x_ref, og_ref, r_ref, od_ref, mk_ref, mv_ref, x1s_ref, aos_ref,
               gg_ref, wo_ref, lg1_ref, lb1_ref, wq_ref, wmo_ref, lg2_ref, lb2_ref, wr_ref, br_ref,
               x2_o, x2p_o, eid_o, ew_o, *, prompt_steps):
    def tail(x1, ao):
        x2 = _layer_norm(ALPHA * x1 + _mm(ao, wmo_ref[...]), lg2_ref[...], lb2_ref[...])
        x2_o[...] = x2
        _pack_pairs(x2p_o, x2)
        _route(x2, wr_ref[...], br_ref[...], eid_o, ew_o)

    i = pl.program_id(0)

    @pl.when(i < prompt_steps)
    def _():
        x1 = _mixer_out(x_ref[...], og_ref[...], r_ref[...], od_ref[...], gg_ref[...], wo_ref[...],
                        lg1_ref[...], lb1_ref[...])
        q = _mm(x1.astype(BF16), wq_ref[...]).astype(BF16)
        scale = MEM_DH ** -0.5
        outs = []
        for h in range(MEM_HEADS):
            hs = slice(h * MEM_DH, (h + 1) * MEM_DH)
            s = _mm_nt(q[:, hs], mk_ref[:, hs].astype(BF16)) * scale
            m = jnp.max(s, axis=-1, keepdims=True)
            p = jnp.exp(s - m)
            p = p / jnp.sum(p, axis=-1, keepdims=True)
            outs.append(_mm(p.astype(BF16), mv_ref[:, hs].astype(BF16)))
        tail(x1, jnp.concatenate(outs, axis=1).astype(BF16))

    @pl.when(i >= prompt_steps)
    def _():
        tail(x1s_ref[...], aos_ref[...])


def _post(x_p, og_p, r_p, od_p, mem_k, mem_v, x1_s, ao_s, gg, wo, lg1, lb1, wq, wmo, lg2, lb2, wr, br):
    n_p, n_s = x_p.shape[0], x1_s.shape[0]
    bsz, mt, _ = mem_k.shape
    tm = int(np.gcd(np.gcd(n_p // bsz, n_s), TOKEN_TILE))
    sp, ss = n_p // tm, n_s // tm
    steps_per_b = sp // bsz
    prow = lambda i: (jnp.minimum(i, sp - 1), 0)
    mrow = lambda i: (jnp.minimum(i, sp - 1) // steps_per_b, 0, 0)
    srow = lambda i: (jnp.maximum(i - sp, 0), 0)
    orow = lambda i: (i, 0)
    const = lambda i: (0, 0)
    full = lambda a: pl.BlockSpec(a.shape, const)
    n_total = n_p + n_s
    weights = (gg, wo, lg1, lb1, wq, wmo, lg2, lb2, wr, br)
    return pl.pallas_call(
        functools.partial(_post_body, prompt_steps=sp),
        out_shape=[jax.ShapeDtypeStruct((n_total, D_MODEL), F32),
                   jax.ShapeDtypeStruct((n_total * PAIR_TILES, LANES), jnp.uint32),
                   jax.ShapeDtypeStruct((n_total, LANES), jnp.int32),
                   jax.ShapeDtypeStruct((n_total, LANES), F32)],
        grid=(sp + ss,),
        in_specs=[pl.BlockSpec((tm, D_MODEL), prow), pl.BlockSpec((tm, GLA_VAL_W), prow),
                  pl.BlockSpec((tm, GLA_VAL_W), prow), pl.BlockSpec((tm, DIL_W), prow),
                  pl.BlockSpec((None, mt, D_MODEL), mrow), pl.BlockSpec((None, mt, D_MODEL), mrow),
                  pl.BlockSpec((tm, D_MODEL), srow), pl.BlockSpec((tm, D_MODEL), srow)]
                 + [full(w) for w in weights],
        out_specs=[pl.BlockSpec((tm, D_MODEL), orow), pl.BlockSpec((tm * PAIR_TILES, LANES), orow),
                   pl.BlockSpec((tm, LANES), orow), pl.BlockSpec((tm, LANES), orow)],
        compiler_params=_params(("arbitrary",)),
        name="post",
    )(x_p, og_p, r_p, od_p, mem_k, mem_v, x1_s, ao_s, *weights)


MOE_DMA_UNROLL = 16
MOE_BURSTS = 8
MXU_COLS = 256
MOE_VMEM_LIMIT = 60 * 1024 * 1024


def _moe_body(te_ref, nv_ref, nlive_ref, src0_ref, srcn_ref, dst0_ref, dstn_ref, spare_ref, xp_hbm,
              wg_ref, wu_ref, wd_ref, eo_hbm,
              xp, gbuf, obuf, src_smem, dst_smem, xsem, ssem, isem, dsem, wgb, wub, wdb, *, nt, spare0):
    t = pl.program_id(0)
    tile = MOE_TILE
    slot = t % 2
    rt = ROW_TILES
    pt = PAIR_TILES
    burst = tile // MOE_BURSTS

    def src_copy(idx_vmem, s):
        return pltpu.make_async_copy(idx_vmem.at[0], src_smem.at[s], isem.at[s])

    def dst_copy(idx_vmem, s):
        return pltpu.make_async_copy(idx_vmem.at[0], dst_smem.at[s], dsem.at[s % 2])

    def scatter_wait(s):
        pltpu.make_async_copy(obuf.at[s], eo_hbm.at[pl.ds(0, tile * rt)], ssem.at[s]).wait()

    def scatter_rows(s, d, lo, hi):
        def chunk(c, carry):
            for u in range(MOE_DMA_UNROLL):
                i = c * MOE_DMA_UNROLL + u
                row = pl.multiple_of(dst_smem[d, 0, i], rt)
                pltpu.make_async_copy(obuf.at[s, pl.ds(pl.multiple_of(i * rt, rt), rt)],
                                      eo_hbm.at[pl.ds(row, rt)], ssem.at[s]).start(priority=u % 2)
            return carry
        lax.fori_loop(lo // MOE_DMA_UNROLL, hi // MOE_DMA_UNROLL, chunk, 0)

    def live(i):
        return (i >= 0) & (i < nt) & (nv_ref[jnp.clip(i, 0, nt - 1)] > 0)

    @pl.when(t == 0)
    def _():
        xcp = pltpu.make_async_copy(xp_hbm, xp, xsem)
        xcp.start()
        obuf[...] = jnp.zeros_like(obuf)
        for s in range(2):
            cp = pltpu.make_async_copy(obuf.at[s], eo_hbm.at[pl.ds((spare0 + s * tile) * rt, tile * rt)], ssem.at[s])
            cp.start()
            cp.wait()
        for cp in (src_copy(src0_ref, 0), dst_copy(dst0_ref, 0), dst_copy(spare_ref, 2)):
            cp.start()
            cp.wait()
        xcp.wait()

    @pl.when(live(t + 1))
    def _():
        src_copy(srcn_ref, 1 - slot).start()
        dst_copy(dstn_ref, (t + 1) % 3).start()

    @pl.when((t == 0) | (te_ref[t] != te_ref[jnp.maximum(t - 1, 0)]))
    def _():
        wgb[...] = wg_ref[...].astype(BF16)
        wub[...] = wu_ref[...].astype(BF16)
        wdb[...] = wd_ref[...].astype(BF16)

    prev_dst = (t + 2) % 3

    for s in range(2):
        @pl.when(live(t) & (slot == s))
        def _(s=s):
            def gather(i, carry):
                row = pl.multiple_of(src_smem[s, 0, i], pt)
                gbuf[pl.ds(pl.multiple_of(i * pt, pt), pt), :] = xp[pl.ds(row, pt), :]
                return carry
            lax.fori_loop(0, tile, gather, 0, unroll=16)

            bursts = iter(range(MOE_BURSTS))

            def burst_prev():
                b = next(bursts)
                for i in range(b * burst, (b + 1) * burst):
                    row = pl.multiple_of(dst_smem[prev_dst, 0, i], rt)
                    pltpu.make_async_copy(obuf.at[1 - s, pl.ds(i * rt, rt)], eo_hbm.at[pl.ds(row, rt)],
                                          ssem.at[1 - s]).start(priority=i % 2)

            xb = _unpack_pairs(gbuf, tile)
            nc = MXU_COLS
            hs = []
            for c in range(EXPERT_HIDDEN // nc):
                cols = slice(c * nc, (c + 1) * nc)
                hg = _mm(xb, wgb[:, cols])
                burst_prev()
                hu = _mm(xb, wub[:, cols])
                burst_prev()
                hs.append((hg * jax.nn.sigmoid(hg) * hu).astype(BF16))
            h = jnp.concatenate(hs, axis=1)

            @pl.when(t >= 1)
            def _():
                scatter_wait(s)

            for c in range(D_MODEL // nc):
                out = _mm(h, wdb[:, c * nc:(c + 1) * nc])
                _to_token_tiles(obuf.at[s], out, col0=c * nc)
                burst_prev()

        @pl.when(live(t - 1) & jnp.logical_not(live(t)) & (slot == s))
        def _(s=s):
            scatter_rows(1 - s, prev_dst, 0, tile)

    @pl.when(live(t + 1))
    def _():
        src_copy(srcn_ref, 1 - slot).wait()
        dst_copy(dstn_ref, (t + 1) % 3).wait()

    @pl.when(t == nt - 1)
    def _():
        n_live = nlive_ref[0]

        @pl.when(n_live >= 1)
        def _():
            scatter_wait((n_live - 1) % 2)

        @pl.when(n_live >= 2)
        def _():
            scatter_wait(n_live % 2)


def _moe(tile_expert, n_valid, src, dst, x2p, wg, wu, wd, n_tok):
    nt = tile_expert.shape[0]
    tile = MOE_TILE
    rt = ROW_TILES
    n_live = jnp.sum((n_valid > 0).astype(jnp.int32)).reshape(1)
    spare0 = 2 * n_tok
    spare1 = ((spare0 + tile + jnp.arange(tile, dtype=jnp.int32)) * rt).reshape(1, 1, tile)
    wspec = lambda shape: pl.BlockSpec((None,) + shape, lambda t, te, nv, nl: (te[t], 0, 0))
    ispec = lambda f: pl.BlockSpec((1, 1, tile), lambda t, te, nv, nl: (f(t), 0, 0))
    first = lambda t: 0
    nxt = lambda t: jnp.minimum(t + 1, nt - 1)
    grid_spec = pltpu.PrefetchScalarGridSpec(
        num_scalar_prefetch=3,
        grid=(nt,),
        in_specs=[ispec(first), ispec(nxt), ispec(first), ispec(nxt), ispec(first),
                  pl.BlockSpec(memory_space=pl.ANY),
                  wspec((D_MODEL, EXPERT_HIDDEN)), wspec((D_MODEL, EXPERT_HIDDEN)),
                  wspec((EXPERT_HIDDEN, D_MODEL))],
        out_specs=pl.BlockSpec(memory_space=pl.ANY),
        scratch_shapes=[pltpu.VMEM(x2p.shape, jnp.uint32),
                        pltpu.VMEM((tile * PAIR_TILES, LANES), jnp.uint32),
                        pltpu.VMEM((2, tile * rt, LANES), F32),
                        pltpu.SMEM((2, 1, tile), jnp.int32),
                        pltpu.SMEM((3, 1, tile), jnp.int32),
                        pltpu.SemaphoreType.DMA,
                        pltpu.SemaphoreType.DMA((2,)),
                        pltpu.SemaphoreType.DMA((2,)),
                        pltpu.SemaphoreType.DMA((2,)),
                        pltpu.VMEM((D_MODEL, EXPERT_HIDDEN), BF16),
                        pltpu.VMEM((D_MODEL, EXPERT_HIDDEN), BF16),
                        pltpu.VMEM((EXPERT_HIDDEN, D_MODEL), BF16)])
    return pl.pallas_call(
        functools.partial(_moe_body, nt=nt, spare0=spare0),
        out_shape=jax.ShapeDtypeStruct(((spare0 + 2 * tile) * rt, LANES), F32),
        grid_spec=grid_spec,
        compiler_params=pltpu.CompilerParams(dimension_semantics=("arbitrary",), vmem_limit_bytes=MOE_VMEM_LIMIT),
        name="moe",
    )(tile_expert, n_valid, n_live, src, src, dst, dst, spare1, x2p, wg, wu, wd)


def _final_body(x_ref, ew_ref, e1_ref, e2_ref, lg_ref, lb_ref, y_ref):
    ew = ew_ref[...]
    n = x_ref.shape[0]
    moe = ew[:, 0:1] * _from_token_tiles(e1_ref, n) + ew[:, 1:2] * _from_token_tiles(e2_ref, n)
    y_ref[...] = _layer_norm(ALPHA * x_ref[...] + moe, lg_ref[...], lb_ref[...])


def _final(x2, ew, eo, lg, lb, row0, t):
    n_tok = x2.shape[0]
    tile = int(np.gcd(np.gcd(row0, t), np.gcd(n_tok, TOKEN_TILE)))
    b0 = row0 // tile
    b1 = n_tok // tile
    const = lambda i: (0, 0)
    blk = lambda off: pl.BlockSpec((tile, D_MODEL), lambda i: (off + i, 0))
    tblk = lambda off: pl.BlockSpec((tile * ROW_TILES, LANES), lambda i: (off + i, 0))
    return pl.pallas_call(
        _final_body,
        out_shape=jax.ShapeDtypeStruct((t, D_MODEL), F32),
        grid=(t // tile,),
        in_specs=[blk(b0), pl.BlockSpec((tile, LANES), lambda i: (b0 + i, 0)), tblk(b0), tblk(b1 + b0),
                  pl.BlockSpec(lg.shape, const), pl.BlockSpec(lb.shape, const)],
        out_specs=blk(0),
        compiler_params=_params(("arbitrary",)),
        name="final",
    )(x2, ew, eo, eo, lg, lb)


def _routing_tables(eid, n_tok):
    tile = MOE_TILE
    n_assign = 2 * n_tok
    assert n_assign % tile == 0
    nt = n_assign // tile + N_EXPERTS + 1
    flat = eid.reshape(-1).astype(jnp.int32)
    experts = jnp.arange(N_EXPERTS, dtype=jnp.int32)
    counts = jnp.sum((flat[:, None] == experts[None, :]).astype(jnp.int32), axis=0)
    pad = (-counts) % tile
    unused = 2 * N_EXPERTS
    pad_keys = jnp.where(jnp.arange(tile, dtype=jnp.int32)[None, :] < pad[:, None],
                         2 * experts[:, None] + 1, unused).reshape(-1)
    keys = jnp.concatenate([2 * flat, pad_keys, jnp.full((tile,), unused, jnp.int32)])
    vals = jnp.concatenate([jnp.arange(n_assign, dtype=jnp.int32),
                            jnp.full(((N_EXPERTS + 1) * tile,), -1, jnp.int32)])
    keys, vals = lax.sort((keys, vals), num_keys=1)
    keys = keys.reshape(nt, tile)
    vals = vals.reshape(nt, tile)
    tile_expert = jnp.minimum(keys[:, 0] // 2, N_EXPERTS - 1)
    real = vals >= 0
    n_valid = jnp.sum(real.astype(jnp.int32), axis=1)
    a = jnp.maximum(vals, 0)
    src = a >> 1
    spare = 2 * n_tok + (jnp.arange(nt, dtype=jnp.int32)[:, None] % 2) * tile + jnp.arange(tile, dtype=jnp.int32)[None, :]
    dst = jnp.where(real, (a & 1) * n_tok + (a >> 1), spare)
    return tile_expert, n_valid, src.reshape(nt, 1, tile), dst.reshape(nt, 1, tile)


def kernel(x_prompt, x_sample, mem_prompt, cache_dil_k, cache_dil_v, state_gla, cache_mem_k, cache_mem_v, w_in, w_gate_lr, b_gate, g_gla_norm, w_out, ln_mix_g, ln_mix_b, w_mem_q, w_mem_k, w_mem_v, w_mem_o, ln_mem_g, ln_mem_b, w_route_group, b_route_group, w_route_expert, b_route_expert, w_exp_gate, w_exp_up, w_exp_down, ln_ffn_g, ln_ffn_b):
    assert w_in.shape[0] == DEPTH
    bp, seq, _ = x_prompt.shape
    db, tdec, _ = x_sample.shape
    buf = cache_dil_k.shape[2]
    mt = mem_prompt.shape[1]
    n_p, n_s = bp * seq, db * tdec
    n_tok = n_p + n_s
    l = 0

    a0 = 2 * GLA_KEY_W + 2 * GLA_VAL_W
    w_main = jnp.concatenate([w_in[l][:, :a0], w_in[l][:, a0 + GATE_RANK:]], axis=1).astype(BF16)
    w_a = jnp.pad(w_in[l][:, a0:a0 + GATE_RANK], ((0, 0), (0, LANES - GATE_RANK))).astype(BF16)
    w_gl = jnp.pad(w_gate_lr[l], ((0, LANES - GATE_RANK), (0, 0))).astype(BF16)
    b_g = b_gate[l][None, :]
    row1 = lambda a: a[l][None, :]
    w_o = w_out[l].astype(BF16)
    w_q = w_mem_q[l].astype(BF16)
    w_k = w_mem_k[l].astype(BF16)
    w_v = w_mem_v[l].astype(BF16)
    w_mo = w_mem_o[l].astype(BF16)
    w_r = jnp.pad(jnp.concatenate([w_route_expert[l], w_route_group[l]], axis=1),
                  ((0, 0), (0, LANES - N_EXPERTS - N_GROUPS))).astype(BF16)
    b_r = jnp.pad(jnp.concatenate([b_route_expert[l], b_route_group[l]]), (0, LANES - N_EXPERTS - N_GROUPS))[None, :]

    def mixer_inputs(x2d, seq_len, keep, q_rows):
        return _proj(x2d, w_main, w_a, w_gl, b_g, seq=seq_len, keep=keep, q_rows=q_rows)

    def post_a(x2d, o_g, r_g, o_d):
        return _post_a(x2d, o_g, r_g, o_d, row1(g_gla_norm), w_o, row1(ln_mix_g), row1(ln_mix_b), w_q)

    xp = x_prompt.reshape(n_p, D_MODEL)
    keep = min(WINDOW_MAX, seq)
    qg, kg, vg, rg, la, qd, kd, vd, kd_rows, vd_rows = mixer_inputs(xp, seq, keep, False)
    sh = lambda t: t.reshape(bp, seq, t.shape[-1])
    s0 = jnp.zeros((bp, H_GLA, GLA_DK, GLA_DV), F32)
    o_g, s_fin_p = _gla(sh(qg), sh(kg), sh(vg), sh(la), s0, chain=True)
    o_d = _dil_prompt(sh(qd), sh(kd), sh(vd))
    mem_k, mem_v = _mem_kv(mem_prompt, w_k, w_v)
    prompt_mix = (xp, o_g.reshape(n_p, -1), rg, o_d.reshape(n_p, -1))
    dk_p = kd_rows.reshape(1, bp, keep, H_DIL, DIL_DH)
    dv_p = vd_rows.reshape(1, bp, keep, H_DIL, DIL_DH)

    xs = x_sample.reshape(n_s, D_MODEL)
    qg, kg, vg, rg, la, _, _, _, kd_rows, vd_rows, qd_rows = mixer_inputs(xs, tdec, tdec, True)
    shs = lambda t: t.reshape(db, tdec, t.shape[-1])
    o_g, s_new = _gla(shs(qg), shs(kg), shs(vg), shs(la), state_gla[l], chain=False)
    head_rows = lambda t: t.reshape(db, tdec * H_DIL, DIL_DH)
    groups = lambda c: c[l].reshape(db, buf // DIL_GROUP, DIL_GROUP * H_DIL, DIL_DH)
    nk, nv, o_d = _dil_sample(head_rows(qd_rows), head_rows(kd_rows), head_rows(vd_rows),
                              groups(cache_dil_k), groups(cache_dil_v))
    x1_s, qm = post_a(xs, o_g.reshape(n_s, -1), rg, o_d.reshape(n_s, DIL_W))
    gm = 8 if db % 8 == 0 else 1

    def cache_rows(c):
        return (c[l].reshape(db, mt, MEM_HEADS, MEM_HALVES, LANES).transpose(0, 1, 3, 2, 4)
                .reshape(db, mt * MEM_HALVES * MEM_HEADS, LANES))

    q_rows = (qm.reshape(db, tdec, MEM_HEADS, MEM_HALVES, LANES).transpose(0, 3, 1, 2, 4)
              .reshape(db, MEM_HALVES * tdec * MEM_HEADS, LANES))
    ao_s = _mem_attn_rows(q_rows, cache_rows(cache_mem_k), cache_rows(cache_mem_v), g=gm)
    ao_s = (ao_s.reshape(db, MEM_HALVES, tdec, MEM_HEADS, LANES).transpose(0, 2, 3, 1, 4).reshape(n_s, D_MODEL))

    x2, x2p, eid, ew = _post(*prompt_mix, mem_k, mem_v, x1_s, ao_s, row1(g_gla_norm), w_o, row1(ln_mix_g),
                             row1(ln_mix_b), w_q, w_mo, row1(ln_mem_g), row1(ln_mem_b), w_r, b_r)
    tile_expert, n_valid, src, dst = _routing_tables(eid[:, :2], n_tok)
    eo = _moe(tile_expert, n_valid, src * PAIR_TILES, dst * ROW_TILES, x2p,
              w_exp_gate[l], w_exp_up[l], w_exp_down[l], n_tok)
    y_p = _final(x2, ew, eo, row1(ln_ffn_g), row1(ln_ffn_b), 0, n_p)
    y_s = _final(x2, ew, eo, row1(ln_ffn_g), row1(ln_ffn_b), n_p, n_s)

    return (y_p.reshape(bp, seq, D_MODEL), y_s.reshape(db, tdec, D_MODEL),
            dk_p, dv_p, s_fin_p[None], mem_k.reshape(1, bp, mt, MEM_HEADS, MEM_DH),
            mem_v.reshape(1, bp, mt, MEM_HEADS, MEM_DH),
            nk.reshape(1, db, buf, H_DIL, DIL_DH), nv.reshape(1, db, buf, H_DIL, DIL_DH), s_new[None])
```

```python
import functools

import numpy as np
import jax
import jax.numpy as jnp
from jax import lax
from jax.experimental import pallas as pl
from jax.experimental.pallas import tpu as pltpu

F32 = jnp.float32
BF16 = jnp.bfloat16

D_MODEL = 1024
H_GLA = 4
GLA_DK = 64
GLA_DV = 128
GLA_KEY_W = H_GLA * GLA_DK
GLA_VAL_W = H_GLA * GLA_DV
GATE_RANK = 16
GATE_TAU = 16.0
GLA_CHUNK = 64
H_DIL = 4
DIL_DH = 128
DIL_W = H_DIL * DIL_DH
DIL_PAIRS = ((128, 1), (512, 4), (2048, 16))
DIL_SPAN = 128
WINDOW_MAX = 2048
MEM_HEADS = 4
MEM_DH = 256
N_GROUPS = 4
EXPERTS_PER_GROUP = 8
N_EXPERTS = N_GROUPS * EXPERTS_PER_GROUP
EXPERT_HIDDEN = 512
DEPTH = 1
ALPHA = (2.0 * DEPTH) ** 0.25
LN_EPS = 1e-5
NEG_INF = -1e30

LANES = 128
VMEM_LIMIT = 56 * 1024 * 1024

TOKEN_TILE = 512
MOE_TILE = 256


def _mm(a, b):
    return jnp.dot(a, b, preferred_element_type=F32)


def _mm_nt(a, b):
    return lax.dot_general(a, b, (((1,), (1,)), ((), ())), preferred_element_type=F32)


def _mm_tn(a, b):
    return lax.dot_general(a, b, (((0,), (0,)), ((), ())), preferred_element_type=F32)


def _layer_norm(v, g, b):
    mu = jnp.mean(v, axis=-1, keepdims=True)
    d = v - mu
    var = jnp.mean(d * d, axis=-1, keepdims=True)
    return d * lax.rsqrt(var + LN_EPS) * g + b


def _params(sem):
    return pltpu.CompilerParams(dimension_semantics=sem, vmem_limit_bytes=VMEM_LIMIT)


ROW_TILES = D_MODEL // LANES


def _to_token_tiles(ref, val, col0=0):
    n = val.shape[0]
    for s in range(val.shape[1] // LANES):
        ref[pl.ds(col0 // LANES + s, n, stride=ROW_TILES), :] = val[:, s * LANES:(s + 1) * LANES]


def _from_token_tiles(ref, n):
    return jnp.concatenate([ref[pl.ds(s, n, stride=ROW_TILES), :] for s in range(ROW_TILES)], axis=1)


PAIR_TILES = D_MODEL // 2 // LANES


def _pack_pairs(ref, val):
    n = val.shape[0]
    half = D_MODEL // 2
    bits = lambda v: pltpu.bitcast(v.astype(BF16).astype(F32), jnp.uint32)
    words = (bits(val[:, :half]) >> 16) | (bits(val[:, half:]) & jnp.uint32(0xFFFF0000))
    for s in range(PAIR_TILES):
        ref[pl.ds(s, n, stride=PAIR_TILES), :] = words[:, s * LANES:(s + 1) * LANES]


def _unpack_pairs(ref, n):
    words = [ref[pl.ds(s, n, stride=PAIR_TILES), :] for s in range(PAIR_TILES)]
    lo = [pltpu.bitcast(w << 16, F32).astype(BF16) for w in words]
    hi = [pltpu.bitcast(w & jnp.uint32(0xFFFF0000), F32).astype(BF16) for w in words]
    return jnp.concatenate(lo + hi, axis=1)


def _shift_copies(cache_hbm, rows_hbm, new_hbm, sem, b):
    db, total, _ = cache_hbm.shape
    fresh = rows_hbm.shape[0] // db
    keep = total - fresh
    return (pltpu.make_async_copy(cache_hbm.at[b, pl.ds(fresh, keep)], new_hbm.at[b, pl.ds(0, keep)], sem),
            pltpu.make_async_copy(rows_hbm.at[pl.ds(b * fresh, fresh)], new_hbm.at[b, pl.ds(keep, fresh)], sem))


def _shift_issue(cache_hbm, rows_hbm, new_hbm, sem, step, n_steps):
    db = cache_hbm.shape[0]
    per = -(-db // n_steps)
    for u in range(per):
        b = step * per + u

        @pl.when(b < db)
        def _():
            for cp in _shift_copies(cache_hbm, rows_hbm, new_hbm, sem, b):
                cp.start()


def _shift_wait(cache_hbm, rows_hbm, new_hbm, sem):
    def body(b, carry):
        for cp in _shift_copies(cache_hbm, rows_hbm, new_hbm, sem, b):
            cp.wait()
        return carry
    lax.fori_loop(0, cache_hbm.shape[0], body, 0)


def _to_head_rows(ref, val):
    n = val.shape[0]
    for h in range(H_DIL):
        ref[pl.ds(h, n, stride=H_DIL), :] = val[:, h * DIL_DH:(h + 1) * DIL_DH]


def _proj_body(x_ref, w_ref, wa_ref, wgl_ref, bg_ref,
               qg_o, kg_o, vg_o, r_o, la_o, qd_o, kd_o, vd_o, *row_outs):
    xb = x_ref[...].astype(BF16)

    def mm(lo, hi):
        return _mm(xb, w_ref[:, lo:hi])

    qg_o[...] = mm(0, 256) * (GLA_DK ** -0.5)
    kg_o[...] = mm(256, 512)
    vg_o[...] = mm(512, 1024)
    r_o[...] = mm(1024, 1536)
    qd = mm(1536, 2048)
    kd = mm(2048, 2560)
    vd = mm(2560, 3072)
    qd_o[...] = qd
    kd_o[...] = kd
    vd_o[...] = vd
    for ref, val in zip(row_outs, (kd, vd, qd)):
        _to_head_rows(ref, val)
    a_lr = _mm(xb, wa_ref[...])
    z = _mm(a_lr.astype(BF16), wgl_ref[...]) + bg_ref[...]
    la_o[...] = (jnp.minimum(z, 0.0) - jnp.log1p(jnp.exp(-jnp.abs(z)))) * (1.0 / GATE_TAU)


def _proj(x, w_main, w_a, w_gl, b_g, *, seq, keep, q_rows):
    t = x.shape[0]
    tm = min(TOKEN_TILE, t)
    widths = (256, 256, 512, 512, 256, 512, 512, 512)
    row = lambda i: (i, 0)
    const = lambda i: (0, 0)
    if keep == seq:
        kept = row
    else:
        assert seq % tm == 0 and keep % tm == 0
        per_seq, per_keep = seq // tm, keep // tm
        kept = lambda i: ((i // per_seq) * per_keep + jnp.maximum(i % per_seq - (per_seq - per_keep), 0), 0)
    n_rows = 3 if q_rows else 2
    n_kept = t // seq * keep
    return pl.pallas_call(
        _proj_body,
        out_shape=[jax.ShapeDtypeStruct((t, w), F32) for w in widths]
                  + [jax.ShapeDtypeStruct((n_kept * H_DIL, DIL_DH), F32)] * n_rows,
        grid=(t // tm,),
        in_specs=[pl.BlockSpec((tm, D_MODEL), row),
                  pl.BlockSpec(w_main.shape, const),
                  pl.BlockSpec(w_a.shape, const),
                  pl.BlockSpec(w_gl.shape, const),
                  pl.BlockSpec(b_g.shape, const)],
        out_specs=[pl.BlockSpec((tm, w), row) for w in widths]
                  + [pl.BlockSpec((tm * H_DIL, DIL_DH), kept)] * n_rows,
        compiler_params=_params(("arbitrary",)),
        name="proj",
    )(x, w_main, w_a, w_gl, b_g)


GLA_ROWS = 256


def _split3(a):
    a1 = a.astype(BF16)
    r1 = a - a1.astype(F32)
    a2 = r1.astype(BF16)
    r2 = r1 - a2.astype(F32)
    return a1, a2, r2.astype(BF16)


def _gla_body(q_ref, k_ref, v_ref, la_ref, s0_ref, o_ref, sfin_ref, s_scr, *, rows, blk, chain):
    nb = rows // blk
    j = pl.program_id(1)
    if chain:
        @pl.when(j == 0)
        def _():
            s_scr[...] = s0_ref[0]

    ri = lax.broadcasted_iota(jnp.int32, (rows, rows), 0)
    ci = lax.broadcasted_iota(jnp.int32, (rows, rows), 1)
    causal = (ci <= ri) & (ri // blk == ci // blk)
    tri = jnp.where(causal, 1.0, 0.0).astype(BF16)
    rblk = lax.broadcasted_iota(jnp.int32, (rows, nb * GLA_DV), 0) // blk
    cblk = lax.broadcasted_iota(jnp.int32, (rows, nb * GLA_DV), 1) // GLA_DV
    own = rblk == cblk

    q = q_ref[...].reshape(rows, GLA_KEY_W)
    k = k_ref[...].reshape(rows, GLA_KEY_W)
    v = v_ref[...].reshape(rows, GLA_VAL_W)
    a = la_ref[...].reshape(rows, GLA_KEY_W)

    cum = _mm(tri, jnp.concatenate(_split3(a), axis=1))
    b = cum[:, :GLA_KEY_W] + cum[:, GLA_KEY_W:2 * GLA_KEY_W] + cum[:, 2 * GLA_KEY_W:]
    b_last = jnp.concatenate([b[(i + 1) * blk - 1:(i + 1) * blk] for i in range(nb)], axis=0)
    b_end = jnp.concatenate([jnp.broadcast_to(b_last[i:i + 1], (blk, GLA_KEY_W)) for i in range(nb)], axis=0)
    q_t = (q * jnp.exp(b)).astype(BF16)
    k_t = (k * jnp.exp(-b)).astype(BF16)
    k_end = (k * jnp.exp(b_end - b)).astype(BF16)
    vb = v.astype(BF16)
    dec = jnp.transpose(jnp.concatenate([jnp.exp(b_last), jnp.zeros((LANES - nb, GLA_KEY_W), F32)], axis=0))

    outs = []
    for h in range(H_GLA):
        hk = slice(h * GLA_DK, (h + 1) * GLA_DK)
        vh = vb[:, h * GLA_DV:(h + 1) * GLA_DV]
        a_in = jnp.where(causal, _mm_nt(q_t[:, hk], k_t[:, hk]), 0.0).astype(BF16)
        o_h = _mm(a_in, vh)
        v_exp = jnp.where(own, jnp.concatenate([vh] * nb, axis=1), jnp.zeros((), BF16))
        kv = _mm_tn(k_end[:, hk], v_exp)
        states = []
        for i in range(nb):
            if chain:
                s_i = s_scr[h] if i == 0 else s_next
            else:
                s_i = s0_ref[i, h]
            states.append(s_i.astype(BF16))
            d_i = jnp.broadcast_to(dec[hk, i:i + 1], (GLA_DK, GLA_DV))
            s_next = d_i * s_i + kv[:, i * GLA_DV:(i + 1) * GLA_DV]
            if not chain:
                sfin_ref[i, h] = s_next
        if chain:
            s_scr[h] = s_next
        inter = _mm(q_t[:, hk], jnp.concatenate(states, axis=1))
        inter = jnp.where(own, inter, 0.0)
        for i in range(nb):
            o_h = o_h + inter[:, i * GLA_DV:(i + 1) * GLA_DV]
        outs.append(o_h)
    o_ref[...] = jnp.concatenate(outs, axis=1).reshape(o_ref.shape)

    if chain:
        @pl.when(j == pl.num_programs(1) - 1)
        def _():
            sfin_ref[0] = s_scr[...]


def _gla(q, k, v, la, s0, *, chain):
    nbat, s, _ = q.shape
    if chain:
        bb, rows, blk = 1, min(GLA_ROWS, s), GLA_CHUNK
        assert s % rows == 0 and rows % blk == 0
    else:
        bb, rows, blk = GLA_CHUNK // s, s, s
        assert GLA_CHUNK % s == 0 and nbat % bb == 0
    tok = lambda i, j: (i, j, 0)
    st = lambda i, j: (i, 0, 0, 0)
    body = functools.partial(_gla_body, rows=bb * rows, blk=blk, chain=chain)
    return pl.pallas_call(
        body,
        out_shape=[jax.ShapeDtypeStruct((nbat, s, GLA_VAL_W), F32),
                   jax.ShapeDtypeStruct((nbat, H_GLA, GLA_DK, GLA_DV), F32)],
        grid=(nbat // bb, s // rows),
        in_specs=[pl.BlockSpec((bb, rows, GLA_KEY_W), tok),
                  pl.BlockSpec((bb, rows, GLA_KEY_W), tok),
                  pl.BlockSpec((bb, rows, GLA_VAL_W), tok),
                  pl.BlockSpec((bb, rows, GLA_KEY_W), tok),
                  pl.BlockSpec((bb, H_GLA, GLA_DK, GLA_DV), st)],
        out_specs=[pl.BlockSpec((bb, rows, GLA_VAL_W), tok),
                   pl.BlockSpec((bb, H_GLA, GLA_DK, GLA_DV), st)],
        scratch_shapes=[pltpu.VMEM((H_GLA, GLA_DK, GLA_DV), F32)],
        compiler_params=_params(("arbitrary", "arbitrary")),
        name="gla",
    )(q, k, v, la, s0)


DIL_QBLK = 128
DIL_TILE = 2048
DIL_UNROLL = 16


def _dil_body(q_ref, k_ref, v_ref, o_ref, acc_ref, m_ref, l_ref, *, tile):
    blk = DIL_QBLK
    t0 = pl.program_id(2) * tile
    qscale = (DIL_DH ** -0.5) * float(np.log2(np.e))
    ri = lax.broadcasted_iota(jnp.int32, (blk, 2 * blk), 0)
    ci = lax.broadcasted_iota(jnp.int32, (blk, 2 * blk), 1)
    band_bias = jnp.where((ci >= ri) & (ci <= ri + DIL_SPAN), 0.0, NEG_INF)
    prev_cols = lax.broadcasted_iota(jnp.int32, (1, 2 * blk), 1) < blk

    def rows(start, dil):
        return pl.ds(start, blk) if dil == 1 else pl.ds(start, blk, stride=dil)

    def block(dil, q0, first, last):
        qr = rows(q0, dil)
        cur0 = t0 + q0
        prev0 = cur0 - dil * blk
        prev_bias = jnp.where(prev_cols & (prev0 < 0), NEG_INF, 0.0)
        prev0 = jnp.maximum(prev0, 0)
        kw = jnp.concatenate([k_ref[rows(prev0, dil), :], k_ref[rows(cur0, dil), :]], axis=0)
        vw = jnp.concatenate([v_ref[rows(prev0, dil), :], v_ref[rows(cur0, dil), :]], axis=0)
        s = _mm_nt((q_ref[qr, :] * qscale).astype(BF16), kw.astype(BF16)) + band_bias + prev_bias
        m_blk = jnp.max(s, axis=-1, keepdims=True)
        p = jnp.exp2(s - m_blk)
        l_blk = jnp.sum(p, axis=-1, keepdims=True)
        acc = _mm(p.astype(BF16), vw.astype(BF16))
        m_new = jnp.broadcast_to(m_blk, (blk, LANES))
        l_new = jnp.broadcast_to(l_blk, (blk, LANES))
        if not first:
            m_old = m_ref[qr, :]
            m_new = jnp.maximum(m_old, m_new)
            a_old = jnp.exp2(m_old - m_new)
            a_blk = jnp.exp2(m_blk - m_new)
            l_new = a_old * l_ref[qr, :] + a_blk * l_blk
            acc = a_old * acc_ref[qr, :] + a_blk * acc
        if last:
            o_ref[qr, :] = acc / l_new
        else:
            acc_ref[qr, :] = acc
            m_ref[qr, :] = m_new
            l_ref[qr, :] = l_new

    dils = sorted((d for _, d in DIL_PAIRS), reverse=True)
    for bi, dil in enumerate(dils):
        per_res = tile // (dil * blk)

        def body(i, carry, dil=dil, per_res=per_res, bi=bi):
            q0 = (i % dil) + (i // dil) * (dil * blk) if dil > 1 else pl.multiple_of(i * blk, blk)
            block(dil, q0, bi == 0, bi == len(dils) - 1)
            return carry

        lax.fori_loop(0, dil * per_res, body, 0, unroll=DIL_UNROLL)


def _dil_prompt(q, k, v):
    bsz, s, _ = q.shape
    tile = min(DIL_TILE, s)
    assert s % tile == 0 and all(tile % (d * DIL_QBLK) == 0 for _, d in DIL_PAIRS)
    assert all(w // d == DIL_SPAN for w, d in DIL_PAIRS)
    qs = pl.BlockSpec((None, tile, DIL_DH), lambda b, h, n: (b, n, h))
    ks = pl.BlockSpec((None, s, DIL_DH), lambda b, h, n: (b, 0, h))
    return pl.pallas_call(
        functools.partial(_dil_body, tile=tile),
        out_shape=jax.ShapeDtypeStruct((bsz, s, DIL_W), F32),
        grid=(bsz, H_DIL, s // tile),
        in_specs=[qs, ks, ks],
        out_specs=qs,
        scratch_shapes=[pltpu.VMEM((tile, DIL_DH), F32), pltpu.VMEM((tile, LANES), F32),
                        pltpu.VMEM((tile, LANES), F32)],
        compiler_params=_params(("arbitrary", "arbitrary", "arbitrary")),
        name="dil_prompt",
    )(q, k, v)


DIL_GROUP = 16
DIL_TAIL = 512


def _dil_sample_counts(buf, t_new):
    ga = (buf - DIL_TAIL) // DIL_GROUP
    pos_all = np.arange(buf)
    sel = (pos_all >= ga * DIL_GROUP) | (pos_all % DIL_GROUP < t_new)
    t = np.arange(t_new)

    def count(pos, real):
        cnt = np.zeros((t_new, pos.size), np.float32)
        for window, dil in DIL_PAIRS:
            delta = buf + t[:, None] - pos[None, :]
            cnt += (delta >= 0) & (delta % dil == 0) & (delta // dil <= window // dil) & real[None, :]
        return cnt

    assert not count(pos_all[~sel], np.ones((~sel).sum(), bool)).any()
    pad_new = LANES // H_DIL
    pos = np.concatenate([pos_all[sel], buf + np.arange(pad_new)])
    real = np.concatenate([np.ones(sel.sum(), bool), np.arange(pad_new) < t_new])
    cnt = count(pos, real)
    same_head = np.eye(H_DIL, dtype=np.float32)
    full = cnt[:, None, :, None] * same_head[None, :, None, :]
    return full.reshape(t_new * H_DIL, pos.size * H_DIL)


def _dil_sample_body(q_ref, kn_ref, vn_ref, km_ref, kt_ref, vm_ref, vt_ref, cnt_ref, o_ref):
    zpad = jnp.zeros((LANES - kn_ref.shape[0], DIL_DH), F32)

    def keys(main_ref, tail_ref, new_ref):
        g, r, _ = main_ref.shape
        gt, rt_, _ = tail_ref.shape
        main = main_ref[...].reshape(g * r, DIL_DH)
        tail = tail_ref[...].reshape(gt * rt_, DIL_DH)
        return jnp.concatenate([main, tail, new_ref[...], zpad], axis=0).astype(BF16)

    s = _mm_nt(q_ref[...].astype(BF16), keys(km_ref, kt_ref, kn_ref)) * (DIL_DH ** -0.5)
    cnt = cnt_ref[...]
    s = jnp.where(cnt > 0.0, s, NEG_INF)
    m = jnp.max(s, axis=-1, keepdims=True)
    p = jnp.exp(s - m) * cnt
    den = jnp.sum(p, axis=-1, keepdims=True)
    o_ref[...] = _mm(p.astype(BF16), keys(vm_ref, vt_ref, vn_ref)) / den


def _dil_sample(q, k_new, v_new, cache_k, cache_v):
    db, rows_new, _ = q.shape
    g, grows = cache_k.shape[1], cache_k.shape[2]
    assert grows == DIL_GROUP * H_DIL and rows_new * 2 == grows
    buf = g * DIL_GROUP
    ga = (buf - DIL_TAIL) // DIL_GROUP
    gt = g - ga
    assert ga % gt == 0
    cnt = jnp.asarray(_dil_sample_counts(buf, rows_new // H_DIL))
    new = pl.BlockSpec((None, rows_new, DIL_DH), lambda b: (b, 0, 0))
    main = pl.BlockSpec((None, ga, rows_new, DIL_DH), lambda b: (b, 0, 0, 0))
    tail = pl.BlockSpec((None, gt, grows, DIL_DH), lambda b: (b, ga // gt, 0, 0))
    return pl.pallas_call(
        _dil_sample_body,
        out_shape=jax.ShapeDtypeStruct((db, rows_new, DIL_DH), F32),
        grid=(db,),
        in_specs=[new, new, new, main, tail, main, tail, pl.BlockSpec(cnt.shape, lambda b: (0, 0))],
        out_specs=new,
        compiler_params=_params(("arbitrary",)),
        name="dil_sample",
    )(q, k_new, v_new, cache_k, cache_k, cache_v, cache_v, cnt)


def _post_a_body(x_ref, og_ref, r_ref, od_ref, gg_ref, wo_ref, lg_ref, lb_ref, wq_ref, x1_o, qm_o):
    x1 = _mixer_out(x_ref[...], og_ref[...], r_ref[...], od_ref[...], gg_ref[...], wo_ref[...],
                    lg_ref[...], lb_ref[...])
    x1_o[...] = x1
    qm_o[...] = _mm(x1.astype(BF16), wq_ref[...]).astype(BF16)


def _post_a(x, og, r, od, gg, wo, lg, lb, wq):
    t = x.shape[0]
    tm = min(TOKEN_TILE, t)
    row = lambda i: (i, 0)
    const = lambda i: (0, 0)
    full = lambda a: pl.BlockSpec(a.shape, const)
    return pl.pallas_call(
        _post_a_body,
        out_shape=[jax.ShapeDtypeStruct((t, D_MODEL), F32), jax.ShapeDtypeStruct((t, D_MODEL), BF16)],
        grid=(t // tm,),
        in_specs=[pl.BlockSpec((tm, D_MODEL), row), pl.BlockSpec((tm, GLA_VAL_W), row),
                  pl.BlockSpec((tm, GLA_VAL_W), row), pl.BlockSpec((tm, DIL_W), row),
                  full(gg), full(wo), full(lg), full(lb), full(wq)],
        out_specs=[pl.BlockSpec((tm, D_MODEL), row), pl.BlockSpec((tm, D_MODEL), row)],
        compiler_params=_params(("arbitrary",)),
        name="post_a",
    )(x, og, r, od, gg, wo, lg, lb, wq)


def _mem_kv_body(m_ref, wk_ref, wv_ref, k_o, v_o):
    mb = m_ref[...].astype(BF16)
    k_o[...] = _mm(mb, wk_ref[...])
    v_o[...] = _mm(mb, wv_ref[...])


def _mem_kv(mem, wk, wv):
    bsz, mt, _ = mem.shape
    blk = pl.BlockSpec((None, mt, D_MODEL), lambda b: (b, 0, 0))
    w = pl.BlockSpec((D_MODEL, D_MODEL), lambda b: (0, 0))
    return pl.pallas_call(
        _mem_kv_body,
        out_shape=[jax.ShapeDtypeStruct((bsz, mt, D_MODEL), F32)] * 2,
        grid=(bsz,),
        in_specs=[blk, w, w],
        out_specs=[blk, blk],
        compiler_params=_params(("arbitrary",)),
        name="mem_kv",
    )(mem, wk, wv)


MEM_HALVES = MEM_DH // LANES


def _mem_attn_rows_body(q_ref, k_ref, v_ref, mask_ref, o_ref, *, g):
    scale = MEM_DH ** -0.5
    hh = MEM_HEADS
    nq = q_ref.shape[1] // MEM_HALVES
    nk = k_ref.shape[1]
    valid = mask_ref[...] > 0.0
    for ig in range(g):
        part = _mm_nt(q_ref[ig], k_ref[ig].astype(BF16))
        s = part[0:nq] + pltpu.roll(part[nq:2 * nq], nk - hh, 1)
        s = jnp.where(valid, s * scale, NEG_INF)
        m = jnp.max(s, axis=-1, keepdims=True)
        p = jnp.exp(s - m)
        p = (p / jnp.sum(p, axis=-1, keepdims=True)).astype(F32)
        both = jnp.concatenate([p, pltpu.roll(p, hh, 1)], axis=0).astype(BF16)
        o_ref[ig] = _mm(both, v_ref[ig].astype(BF16)).astype(BF16)


def _mem_attn_rows(q, mk, mv, *, g):
    nb, qr, _ = q.shape
    kr = mk.shape[1]
    nq = qr // MEM_HALVES
    col = np.arange(kr)
    mask = ((col[None, :] % MEM_HEADS == np.arange(nq)[:, None] % MEM_HEADS)
            & (col[None, :] % (MEM_HEADS * MEM_HALVES) < MEM_HEADS)).astype(np.float32)
    qs = pl.BlockSpec((g, qr, LANES), lambda i: (i, 0, 0))
    ks = pl.BlockSpec((g, kr, LANES), lambda i: (i, 0, 0))
    return pl.pallas_call(
        functools.partial(_mem_attn_rows_body, g=g),
        out_shape=jax.ShapeDtypeStruct((nb, qr, LANES), BF16),
        grid=(nb // g,),
        in_specs=[qs, ks, ks, pl.BlockSpec(mask.shape, lambda i: (0, 0))],
        out_specs=qs,
        compiler_params=_params(("arbitrary",)),
        name="mem_attn_rows",
    )(q, mk, mv, jnp.asarray(mask))


ROUTE_GROUP_LANE0 = N_EXPERTS


def _mixer_out(x, og, r, od, gg, wo, lg, lb):
    parts = []
    for h in range(H_GLA):
        oh = og[:, h * GLA_DV:(h + 1) * GLA_DV]
        parts.append(oh * lax.rsqrt(jnp.mean(oh * oh, axis=-1, keepdims=True) + LN_EPS))
    on = jnp.concatenate(parts, axis=1) * gg * (r * jax.nn.sigmoid(r))
    cat = jnp.concatenate([on.astype(BF16), od.astype(BF16)], axis=1)
    return _layer_norm(ALPHA * x + _mm(cat, wo), lg, lb)


def _route(x2, wr, br, eid_o, ew_o):
    logits = _mm(x2.astype(BF16), wr) + br
    lane = lax.broadcasted_iota(jnp.int32, logits.shape, 1).astype(F32)
    big = float(LANES)

    def first_argmax(vals, vmax):
        return jnp.min(jnp.where(vals == vmax, lane, big), axis=-1, keepdims=True)

    is_g = (lane >= ROUTE_GROUP_LANE0) & (lane < ROUTE_GROUP_LANE0 + N_GROUPS)
    gl = jnp.where(is_g, logits, NEG_INF)
    gmax = jnp.max(gl, axis=-1, keepdims=True)
    g_sel = first_argmax(gl, gmax) - ROUTE_GROUP_LANE0
    g_w = 1.0 / jnp.sum(jnp.where(is_g, jnp.exp(gl - gmax), 0.0), axis=-1, keepdims=True)
    lo = g_sel * EXPERTS_PER_GROUP
    el = jnp.where((lane >= lo) & (lane < lo + EXPERTS_PER_GROUP), logits, NEG_INF)
    v1 = jnp.max(el, axis=-1, keepdims=True)
    i1 = first_argmax(el, v1)
    el2 = jnp.where(lane == i1, NEG_INF, el)
    v2 = jnp.max(el2, axis=-1, keepdims=True)
    i2 = first_argmax(el2, v2)
    e = jnp.exp(v2 - v1)
    w1 = g_w / (1.0 + e)
    w2 = g_w * e / (1.0 + e)
    eid_o[...] = jnp.where(lane == 0.0, i1, jnp.where(lane == 1.0, i2, 0.0)).astype(jnp.int32)
    ew_o[...] = jnp.where(lane == 0.0, w1, jnp.where(lane == 1.0, w2, 0.0))


def _post_body(x_ref, og_ref, r_ref, od_ref, mk_ref, mv_ref, x1s_ref, aos_ref,
               gg_ref, wo_ref, lg1_ref, lb1_ref, wq_ref, wmo_ref, lg2_ref, lb2_ref, wr_ref, br_ref,
               cache_hbm, rows_hbm, x2_o, x2p_o, eid_o, ew_o, new_hbm, csem, *, prompt_steps):
    def tail(x1, ao):
        x2 = _layer_norm(ALPHA * x1 + _mm(ao, wmo_ref[...]), lg2_ref[...], lb2_ref[...])
        x2_o[...] = x2
        _pack_pairs(x2p_o, x2)
        _route(x2, wr_ref[...], br_ref[...], eid_o, ew_o)

    i = pl.program_id(0)
    n_steps = pl.num_programs(0)
    _shift_issue(cache_hbm, rows_hbm, new_hbm, csem, i, n_steps)

    @pl.when(i < prompt_steps)
    def _():
        x1 = _mixer_out(x_ref[...], og_ref[...], r_ref[...], od_ref[...], gg_ref[...], wo_ref[...],
                        lg1_ref[...], lb1_ref[...])
        q = _mm(x1.astype(BF16), wq_ref[...]).astype(BF16)
        scale = MEM_DH ** -0.5
        outs = []
        for h in range(MEM_HEADS):
            hs = slice(h * MEM_DH, (h + 1) * MEM_DH)
            s = _mm_nt(q[:, hs], mk_ref[:, hs].astype(BF16)) * scale
            m = jnp.max(s, axis=-1, keepdims=True)
            p = jnp.exp(s - m)
            p = p / jnp.sum(p, axis=-1, keepdims=True)
            outs.append(_mm(p.astype(BF16), mv_ref[:, hs].astype(BF16)))
        tail(x1, jnp.concatenate(outs, axis=1).astype(BF16))

    @pl.when(i >= prompt_steps)
    def _():
        tail(x1s_ref[...], aos_ref[...])

    @pl.when(i == n_steps - 1)
    def _():
        _shift_wait(cache_hbm, rows_hbm, new_hbm, csem)


def _post(x_p, og_p, r_p, od_p, mem_k, mem_v, x1_s, ao_s, gg, wo, lg1, lb1, wq, wmo, lg2, lb2, wr, br,
          cache, rows):
    n_p, n_s = x_p.shape[0], x1_s.shape[0]
    bsz, mt, _ = mem_k.shape
    tm = int(np.gcd(np.gcd(n_p // bsz, n_s), TOKEN_TILE))
    sp, ss = n_p // tm, n_s // tm
    steps_per_b = sp // bsz
    prow = lambda i: (jnp.minimum(i, sp - 1), 0)
    mrow = lambda i: (jnp.minimum(i, sp - 1) // steps_per_b, 0, 0)
    srow = lambda i: (jnp.maximum(i - sp, 0), 0)
    orow = lambda i: (i, 0)
    const = lambda i: (0, 0)
    full = lambda a: pl.BlockSpec(a.shape, const)
    n_total = n_p + n_s
    weights = (gg, wo, lg1, lb1, wq, wmo, lg2, lb2, wr, br)
    return pl.pallas_call(
        functools.partial(_post_body, prompt_steps=sp),
        out_shape=[jax.ShapeDtypeStruct((n_total, D_MODEL), F32),
                   jax.ShapeDtypeStruct((n_total * PAIR_TILES, LANES), jnp.uint32),
                   jax.ShapeDtypeStruct((n_total, LANES), jnp.int32),
                   jax.ShapeDtypeStruct((n_total, LANES), F32),
                   jax.ShapeDtypeStruct(cache.shape, cache.dtype)],
        grid=(sp + ss,),
        in_specs=[pl.BlockSpec((tm, D_MODEL), prow), pl.BlockSpec((tm, GLA_VAL_W), prow),
                  pl.BlockSpec((tm, GLA_VAL_W), prow), pl.BlockSpec((tm, DIL_W), prow),
                  pl.BlockSpec((None, mt, D_MODEL), mrow), pl.BlockSpec((None, mt, D_MODEL), mrow),
                  pl.BlockSpec((tm, D_MODEL), srow), pl.BlockSpec((tm, D_MODEL), srow)]
                 + [full(w) for w in weights] + [pl.BlockSpec(memory_space=pl.ANY)] * 2,
        out_specs=[pl.BlockSpec((tm, D_MODEL), orow), pl.BlockSpec((tm * PAIR_TILES, LANES), orow),
                   pl.BlockSpec((tm, LANES), orow), pl.BlockSpec((tm, LANES), orow),
                   pl.BlockSpec(memory_space=pl.ANY)],
        scratch_shapes=[pltpu.SemaphoreType.DMA],
        compiler_params=_params(("arbitrary",)),
        name="post",
    )(x_p, og_p, r_p, od_p, mem_k, mem_v, x1_s, ao_s, *weights, cache, rows)


MOE_DMA_UNROLL = 16
MOE_BURSTS = 8
MXU_COLS = 256
MOE_VMEM_LIMIT = 60 * 1024 * 1024


def _moe_body(te_ref, nv_ref, nlive_ref, src0_ref, srcn_ref, dst0_ref, dstn_ref, spare_ref, xp_hbm,
              wg_ref, wu_ref, wd_ref, cache_hbm, rows_hbm, eo_hbm, new_hbm,
              xp, gbuf, obuf, src_smem, dst_smem, xsem, ssem, isem, dsem, csem, wgb, wub, wdb, *, nt, spare0):
    t = pl.program_id(0)
    _shift_issue(cache_hbm, rows_hbm, new_hbm, csem, t, nt)
    tile = MOE_TILE
    slot = t % 2
    rt = ROW_TILES
    pt = PAIR_TILES
    burst = tile // MOE_BURSTS

    def src_copy(idx_vmem, s):
        return pltpu.make_async_copy(idx_vmem.at[0], src_smem.at[s], isem.at[s])

    def dst_copy(idx_vmem, s):
        return pltpu.make_async_copy(idx_vmem.at[0], dst_smem.at[s], dsem.at[s % 2])

    def scatter_wait(s):
        pltpu.make_async_copy(obuf.at[s], eo_hbm.at[pl.ds(0, tile * rt)], ssem.at[s]).wait()

    def scatter_rows(s, d, lo, hi):
        def chunk(c, carry):
            for u in range(MOE_DMA_UNROLL):
                i = c * MOE_DMA_UNROLL + u
                row = pl.multiple_of(dst_smem[d, 0, i], rt)
                pltpu.make_async_copy(obuf.at[s, pl.ds(pl.multiple_of(i * rt, rt), rt)],
                                      eo_hbm.at[pl.ds(row, rt)], ssem.at[s]).start(priority=u % 2)
            return carry
        lax.fori_loop(lo // MOE_DMA_UNROLL, hi // MOE_DMA_UNROLL, chunk, 0)

    def live(i):
        return (i >= 0) & (i < nt) & (nv_ref[jnp.clip(i, 0, nt - 1)] > 0)

    @pl.when(t == 0)
    def _():
        xcp = pltpu.make_async_copy(xp_hbm, xp, xsem)
        xcp.start()
        obuf[...] = jnp.zeros_like(obuf)
        for s in range(2):
            cp = pltpu.make_async_copy(obuf.at[s], eo_hbm.at[pl.ds((spare0 + s * tile) * rt, tile * rt)], ssem.at[s])
            cp.start()
            cp.wait()
        for cp in (src_copy(src0_ref, 0), dst_copy(dst0_ref, 0), dst_copy(spare_ref, 2)):
            cp.start()
            cp.wait()
        xcp.wait()

    @pl.when(live(t + 1))
    def _():
        src_copy(srcn_ref, 1 - slot).start()
        dst_copy(dstn_ref, (t + 1) % 3).start()

    @pl.when((t == 0) | (te_ref[t] != te_ref[jnp.maximum(t - 1, 0)]))
    def _():
        wgb[...] = wg_ref[...].astype(BF16)
        wub[...] = wu_ref[...].astype(BF16)
        wdb[...] = wd_ref[...].astype(BF16)

    prev_dst = (t + 2) % 3

    for s in range(2):
        @pl.when(live(t) & (slot == s))
        def _(s=s):
            def gather(i, carry):
                row = pl.multiple_of(src_smem[s, 0, i], pt)
                gbuf[pl.ds(pl.multiple_of(i * pt, pt), pt), :] = xp[pl.ds(row, pt), :]
                return carry
            lax.fori_loop(0, tile, gather, 0, unroll=16)

            bursts = iter(range(MOE_BURSTS))

            def burst_prev():
                b = next(bursts)
                for i in range(b * burst, (b + 1) * burst):
                    row = pl.multiple_of(dst_smem[prev_dst, 0, i], rt)
                    pltpu.make_async_copy(obuf.at[1 - s, pl.ds(i * rt, rt)], eo_hbm.at[pl.ds(row, rt)],
                                          ssem.at[1 - s]).start(priority=i % 2)

            xb = _unpack_pairs(gbuf, tile)
            nc = MXU_COLS
            hs = []
            for c in range(EXPERT_HIDDEN // nc):
                cols = slice(c * nc, (c + 1) * nc)
                hg = _mm(xb, wgb[:, cols])
                burst_prev()
                hu = _mm(xb, wub[:, cols])
                burst_prev()
                hs.append((hg * jax.nn.sigmoid(hg) * hu).astype(BF16))
            h = jnp.concatenate(hs, axis=1)

            @pl.when(t >= 1)
            def _():
                scatter_wait(s)

            for c in range(D_MODEL // nc):
                out = _mm(h, wdb[:, c * nc:(c + 1) * nc])
                _to_token_tiles(obuf.at[s], out, col0=c * nc)
                burst_prev()

        @pl.when(live(t - 1) & jnp.logical_not(live(t)) & (slot == s))
        def _(s=s):
            scatter_rows(1 - s, prev_dst, 0, tile)

    @pl.when(live(t + 1))
    def _():
        src_copy(srcn_ref, 1 - slot).wait()
        dst_copy(dstn_ref, (t + 1) % 3).wait()

    @pl.when(t == nt - 1)
    def _():
        n_live = nlive_ref[0]

        @pl.when(n_live >= 1)
        def _():
            scatter_wait((n_live - 1) % 2)

        @pl.when(n_live >= 2)
        def _():
            scatter_wait(n_live % 2)

        _shift_wait(cache_hbm, rows_hbm, new_hbm, csem)


def _moe(tile_expert, n_valid, src, dst, x2p, wg, wu, wd, n_tok, cache, rows):
    nt = tile_expert.shape[0]
    tile = MOE_TILE
    rt = ROW_TILES
    n_live = jnp.sum((n_valid > 0).astype(jnp.int32)).reshape(1)
    spare0 = 2 * n_tok
    spare1 = ((spare0 + tile + jnp.arange(tile, dtype=jnp.int32)) * rt).reshape(1, 1, tile)
    wspec = lambda shape: pl.BlockSpec((None,) + shape, lambda t, te, nv, nl: (te[t], 0, 0))
    ispec = lambda f: pl.BlockSpec((1, 1, tile), lambda t, te, nv, nl: (f(t), 0, 0))
    first = lambda t: 0
    nxt = lambda t: jnp.minimum(t + 1, nt - 1)
    grid_spec = pltpu.PrefetchScalarGridSpec(
        num_scalar_prefetch=3,
        grid=(nt,),
        in_specs=[ispec(first), ispec(nxt), ispec(first), ispec(nxt), ispec(first),
                  pl.BlockSpec(memory_space=pl.ANY),
                  wspec((D_MODEL, EXPERT_HIDDEN)), wspec((D_MODEL, EXPERT_HIDDEN)),
                  wspec((EXPERT_HIDDEN, D_MODEL)),
                  pl.BlockSpec(memory_space=pl.ANY), pl.BlockSpec(memory_space=pl.ANY)],
        out_specs=[pl.BlockSpec(memory_space=pl.ANY), pl.BlockSpec(memory_space=pl.ANY)],
        scratch_shapes=[pltpu.VMEM(x2p.shape, jnp.uint32),
                        pltpu.VMEM((tile * PAIR_TILES, LANES), jnp.uint32),
                        pltpu.VMEM((2, tile * rt, LANES), F32),
                        pltpu.SMEM((2, 1, tile), jnp.int32),
                        pltpu.SMEM((3, 1, tile), jnp.int32),
                        pltpu.SemaphoreType.DMA,
                        pltpu.SemaphoreType.DMA((2,)),
                        pltpu.SemaphoreType.DMA((2,)),
                        pltpu.SemaphoreType.DMA((2,)),
                        pltpu.SemaphoreType.DMA,
                        pltpu.VMEM((D_MODEL, EXPERT_HIDDEN), BF16),
                        pltpu.VMEM((D_MODEL, EXPERT_HIDDEN), BF16),
                        pltpu.VMEM((EXPERT_HIDDEN, D_MODEL), BF16)])
    return pl.pallas_call(
        functools.partial(_moe_body, nt=nt, spare0=spare0),
        out_shape=[jax.ShapeDtypeStruct(((spare0 + 2 * tile) * rt, LANES), F32),
                   jax.ShapeDtypeStruct(cache.shape, cache.dtype)],
        grid_spec=grid_spec,
        compiler_params=pltpu.CompilerParams(dimension_semantics=("arbitrary",), vmem_limit_bytes=MOE_VMEM_LIMIT),
        name="moe",
    )(tile_expert, n_valid, n_live, src, src, dst, dst, spare1, x2p, wg, wu, wd, cache, rows)


def _final_body(x_ref, ew_ref, e1_ref, e2_ref, lg_ref, lb_ref, y_ref):
    ew = ew_ref[...]
    n = x_ref.shape[0]
    moe = ew[:, 0:1] * _from_token_tiles(e1_ref, n) + ew[:, 1:2] * _from_token_tiles(e2_ref, n)
    y_ref[...] = _layer_norm(ALPHA * x_ref[...] + moe, lg_ref[...], lb_ref[...])


def _final(x2, ew, eo, lg, lb, row0, t):
    n_tok = x2.shape[0]
    tile = int(np.gcd(np.gcd(row0, t), np.gcd(n_tok, TOKEN_TILE)))
    b0 = row0 // tile
    b1 = n_tok // tile
    const = lambda i: (0, 0)
    blk = lambda off: pl.BlockSpec((tile, D_MODEL), lambda i: (off + i, 0))
    tblk = lambda off: pl.BlockSpec((tile * ROW_TILES, LANES), lambda i: (off + i, 0))
    return pl.pallas_call(
        _final_body,
        out_shape=jax.ShapeDtypeStruct((t, D_MODEL), F32),
        grid=(t // tile,),
        in_specs=[blk(b0), pl.BlockSpec((tile, LANES), lambda i: (b0 + i, 0)), tblk(b0), tblk(b1 + b0),
                  pl.BlockSpec(lg.shape, const), pl.BlockSpec(lb.shape, const)],
        out_specs=blk(0),
        compiler_params=_params(("arbitrary",)),
        name="final",
    )(x2, ew, eo, eo, lg, lb)


def _routing_tables(eid, n_tok):
    tile = MOE_TILE
    n_assign = 2 * n_tok
    assert n_assign % tile == 0
    nt = n_assign // tile + N_EXPERTS + 1
    flat = eid.reshape(-1).astype(jnp.int32)
    experts = jnp.arange(N_EXPERTS, dtype=jnp.int32)
    counts = jnp.sum((flat[:, None] == experts[None, :]).astype(jnp.int32), axis=0)
    pad = (-counts) % tile
    unused = 2 * N_EXPERTS
    pad_keys = jnp.where(jnp.arange(tile, dtype=jnp.int32)[None, :] < pad[:, None],
                         2 * experts[:, None] + 1, unused).reshape(-1)
    keys = jnp.concatenate([2 * flat, pad_keys, jnp.full((tile,), unused, jnp.int32)])
    vals = jnp.concatenate([jnp.arange(n_assign, dtype=jnp.int32),
                            jnp.full(((N_EXPERTS + 1) * tile,), -1, jnp.int32)])
    keys, vals = lax.sort((keys, vals), num_keys=1)
    keys = keys.reshape(nt, tile)
    vals = vals.reshape(nt, tile)
    tile_expert = jnp.minimum(keys[:, 0] // 2, N_EXPERTS - 1)
    real = vals >= 0
    n_valid = jnp.sum(real.astype(jnp.int32), axis=1)
    a = jnp.maximum(vals, 0)
    src = a >> 1
    spare = 2 * n_tok + (jnp.arange(nt, dtype=jnp.int32)[:, None] % 2) * tile + jnp.arange(tile, dtype=jnp.int32)[None, :]
    dst = jnp.where(real, (a & 1) * n_tok + (a >> 1), spare)
    return tile_expert, n_valid, src.reshape(nt, 1, tile), dst.reshape(nt, 1, tile)


def kernel(x_prompt, x_sample, mem_prompt, cache_dil_k, cache_dil_v, state_gla, cache_mem_k, cache_mem_v, w_in, w_gate_lr, b_gate, g_gla_norm, w_out, ln_mix_g, ln_mix_b, w_mem_q, w_mem_k, w_mem_v, w_mem_o, ln_mem_g, ln_mem_b, w_route_group, b_route_group, w_route_expert, b_route_expert, w_exp_gate, w_exp_up, w_exp_down, ln_ffn_g, ln_ffn_b):
    assert w_in.shape[0] == DEPTH
    bp, seq, _ = x_prompt.shape
    db, tdec, _ = x_sample.shape
    buf = cache_dil_k.shape[2]
    mt = mem_prompt.shape[1]
    n_p, n_s = bp * seq, db * tdec
    n_tok = n_p + n_s
    l = 0

    a0 = 2 * GLA_KEY_W + 2 * GLA_VAL_W
    w_main = jnp.concatenate([w_in[l][:, :a0], w_in[l][:, a0 + GATE_RANK:]], axis=1).astype(BF16)
    w_a = jnp.pad(w_in[l][:, a0:a0 + GATE_RANK], ((0, 0), (0, LANES - GATE_RANK))).astype(BF16)
    w_gl = jnp.pad(w_gate_lr[l], ((0, LANES - GATE_RANK), (0, 0))).astype(BF16)
    b_g = b_gate[l][None, :]
    row1 = lambda a: a[l][None, :]
    w_o = w_out[l].astype(BF16)
    w_q = w_mem_q[l].astype(BF16)
    w_k = w_mem_k[l].astype(BF16)
    w_v = w_mem_v[l].astype(BF16)
    w_mo = w_mem_o[l].astype(BF16)
    w_r = jnp.pad(jnp.concatenate([w_route_expert[l], w_route_group[l]], axis=1),
                  ((0, 0), (0, LANES - N_EXPERTS - N_GROUPS))).astype(BF16)
    b_r = jnp.pad(jnp.concatenate([b_route_expert[l], b_route_group[l]]), (0, LANES - N_EXPERTS - N_GROUPS))[None, :]

    def mixer_inputs(x2d, seq_len, keep, q_rows):
        return _proj(x2d, w_main, w_a, w_gl, b_g, seq=seq_len, keep=keep, q_rows=q_rows)

    def post_a(x2d, o_g, r_g, o_d):
        return _post_a(x2d, o_g, r_g, o_d, row1(g_gla_norm), w_o, row1(ln_mix_g), row1(ln_mix_b), w_q)

    xp = x_prompt.reshape(n_p, D_MODEL)
    keep = min(WINDOW_MAX, seq)
    qg, kg, vg, rg, la, qd, kd, vd, kd_rows, vd_rows = mixer_inputs(xp, seq, keep, False)
    sh = lambda t: t.reshape(bp, seq, t.shape[-1])
    s0 = jnp.zeros((bp, H_GLA, GLA_DK, GLA_DV), F32)
    o_g, s_fin_p = _gla(sh(qg), sh(kg), sh(vg), sh(la), s0, chain=True)
    o_d = _dil_prompt(sh(qd), sh(kd), sh(vd))
    mem_k, mem_v = _mem_kv(mem_prompt, w_k, w_v)
    prompt_mix = (xp, o_g.reshape(n_p, -1), rg, o_d.reshape(n_p, -1))
    dk_p = kd_rows.reshape(1, bp, keep, H_DIL, DIL_DH)
    dv_p = vd_rows.reshape(1, bp, keep, H_DIL, DIL_DH)

    xs = x_sample.reshape(n_s, D_MODEL)
    qg, kg, vg, rg, la, _, _, _, kd_rows, vd_rows, qd_rows = mixer_inputs(xs, tdec, tdec, True)
    shs = lambda t: t.reshape(db, tdec, t.shape[-1])
    o_g, s_new = _gla(shs(qg), shs(kg), shs(vg), shs(la), state_gla[l], chain=False)
    head_rows = lambda t: t.reshape(db, tdec * H_DIL, DIL_DH)
    groups = lambda c: c[l].reshape(db, buf // DIL_GROUP, DIL_GROUP * H_DIL, DIL_DH)
    o_d = _dil_sample(head_rows(qd_rows), head_rows(kd_rows), head_rows(vd_rows),
                      groups(cache_dil_k), groups(cache_dil_v))
    flat = lambda c: c[l].reshape(db, buf * H_DIL, DIL_DH)
    x1_s, qm = post_a(xs, o_g.reshape(n_s, -1), rg, o_d.reshape(n_s, DIL_W))
    gm = 8 if db % 8 == 0 else 1

    def cache_rows(c):
        return (c[l].reshape(db, mt, MEM_HEADS, MEM_HALVES, LANES).transpose(0, 1, 3, 2, 4)
                .reshape(db, mt * MEM_HALVES * MEM_HEADS, LANES))

    q_rows = (qm.reshape(db, tdec, MEM_HEADS, MEM_HALVES, LANES).transpose(0, 3, 1, 2, 4)
              .reshape(db, MEM_HALVES * tdec * MEM_HEADS, LANES))
    ao_s = _mem_attn_rows(q_rows, cache_rows(cache_mem_k), cache_rows(cache_mem_v), g=gm)
    ao_s = (ao_s.reshape(db, MEM_HALVES, tdec, MEM_HEADS, LANES).transpose(0, 2, 3, 1, 4).reshape(n_s, D_MODEL))

    x2, x2p, eid, ew, nk = _post(*prompt_mix, mem_k, mem_v, x1_s, ao_s, row1(g_gla_norm), w_o, row1(ln_mix_g),
                                 row1(ln_mix_b), w_q, w_mo, row1(ln_mem_g), row1(ln_mem_b), w_r, b_r,
                                 flat(cache_dil_k), kd_rows)
    tile_expert, n_valid, src, dst = _routing_tables(eid[:, :2], n_tok)
    eo, nv = _moe(tile_expert, n_valid, src * PAIR_TILES, dst * ROW_TILES, x2p,
                  w_exp_gate[l], w_exp_up[l], w_exp_down[l], n_tok, flat(cache_dil_v), vd_rows)
    y_p = _final(x2, ew, eo, row1(ln_ffn_g), row1(ln_ffn_b), 0, n_p)
    y_s = _final(x2, ew, eo, row1(ln_ffn_g), row1(ln_ffn_b), n_p, n_s)

    return (y_p.reshape(bp, seq, D_MODEL), y_s.reshape(db, tdec, D_MODEL),
            dk_p, dv_p, s_fin_p[None], mem_k.reshape(1, bp, mt, MEM_HEADS, MEM_DH),
            mem_v.reshape(1, bp, mt, MEM_HEADS, MEM_DH),
            nk.reshape(1, db, buf, H_DIL, DIL_DH), nv.reshape(1, db, buf, H_DIL, DIL_DH), s_new[None])
```
